```python
import jax, jax.numpy as jnp
from jax import lax
import numpy as np

D_MODEL = 2048
BATCH = 4
SEQ = 8192
DEPTH = 1

CHUNK = 64
D_MIX = D_MODEL
D_A = D_MIX // 2
D_B = D_MIX - D_A
GROUP_WIDTH = 128
N_GROUPS_A = D_A // GROUP_WIDTH
N_GROUPS_B = D_B // GROUP_WIDTH
CONV_A_WIDTH = 31
CONV_B_WIDTH = 3
CONV_FFN_WIDTH = 3
D_FF = 5632
PLE_DIM = 256
D_IN_PROJ = 2 * D_A + 3 * D_B
EPS = 1e-6

kernel_name = "hybrid_conformer_shortconv_block"


def rmsnorm(x, g):
    xf = x.astype(jnp.float32)
    y = xf * lax.rsqrt(jnp.mean(xf * xf, axis=-1, keepdims=True) + EPS)
    return (y * g.astype(jnp.float32)).astype(x.dtype)


def layernorm(x, g, b):
    xf = x.astype(jnp.float32)
    mu = jnp.mean(xf, axis=-1, keepdims=True)
    var = jnp.mean(jnp.square(xf - mu), axis=-1, keepdims=True)
    y = (xf - mu) * lax.rsqrt(var + EPS)
    return (y * g.astype(jnp.float32) + b.astype(jnp.float32)).astype(x.dtype)


def causal_dwconv(x, w):
    k, c = w.shape
    return lax.conv_general_dilated(
        x, w[:, None, :].astype(x.dtype),
        window_strides=(1,), padding=[(k - 1, 0)],
        dimension_numbers=("NWC", "WIO", "NWC"),
        feature_group_count=c)


def setup_inputs(seed: int = 0) -> dict:
    key = jax.random.key(seed)
    ks = jax.random.split(key, 20)
    f32 = jnp.float32
    nrm = lambda k, shape, scale: jax.random.normal(k, shape, f32) * scale
    return {
        "x": nrm(ks[0], (BATCH, SEQ, D_MODEL), 1.0),
        "p": nrm(ks[1], (DEPTH, BATCH, SEQ, PLE_DIM), 1.0),
        "norm_mix_g": 1.0 + nrm(ks[2], (DEPTH, D_MODEL), 0.02),
        "w_in": nrm(ks[3], (DEPTH, D_MODEL, D_IN_PROJ), D_MODEL ** -0.5),
        "conv_a_w": nrm(ks[4], (DEPTH, CONV_A_WIDTH, D_A), CONV_A_WIDTH ** -0.5),
        "conv_a_b": nrm(ks[5], (DEPTH, D_A), 0.02),
        "ln_a_g": 1.0 + nrm(ks[6], (DEPTH, D_A), 0.02),
        "ln_a_b": nrm(ks[7], (DEPTH, D_A), 0.02),
        "conv_b_w": nrm(ks[8], (DEPTH, CONV_B_WIDTH, D_B), CONV_B_WIDTH ** -0.5),
        "w_out": nrm(ks[9], (DEPTH, D_MIX, D_MODEL), D_MIX ** -0.5),
        "norm_ffn_g": 1.0 + nrm(ks[10], (DEPTH, D_MODEL), 0.02),
        "w_up": nrm(ks[11], (DEPTH, D_MODEL, 2 * D_FF), D_MODEL ** -0.5),
        "conv_ffn_w": nrm(ks[12], (DEPTH, CONV_FFN_WIDTH, 2 * D_FF), CONV_FFN_WIDTH ** -0.5),
        "w_down": nrm(ks[13], (DEPTH, D_FF, D_MODEL), D_FF ** -0.5),
        "w_ple_gate": nrm(ks[14], (DEPTH, D_MODEL, D_MODEL), D_MODEL ** -0.5),
        "b_ple_gate": nrm(ks[15], (DEPTH, D_MODEL), 0.02),
        "w_ple_proj": nrm(ks[16], (DEPTH, PLE_DIM, D_MODEL), PLE_DIM ** -0.5),
        "norm_final_g": 1.0 + nrm(ks[17], (D_MODEL,), 0.02),
    }


def reference(x, p, norm_mix_g, w_in, conv_a_w, conv_a_b, ln_a_g, ln_a_b, conv_b_w,
              w_out, norm_ffn_g, w_up, conv_ffn_w, w_down, w_ple_gate, b_ple_gate,
              w_ple_proj, norm_final_g):
    h = x
    split_pts = [D_A, 2 * D_A, 2 * D_A + D_B, 2 * D_A + 2 * D_B]
    for i in range(DEPTH):
        hn = rmsnorm(h, norm_mix_g[i])
        z = jnp.einsum("bsd,de->bse", hn, w_in[i])
        a_val, a_gate, b_gate, c_gate, b_h = jnp.split(z, split_pts, axis=-1)
        a = a_val * jax.nn.sigmoid(a_gate)
        a = causal_dwconv(a, conv_a_w[i]) + conv_a_b[i]
        a = jax.nn.silu(layernorm(a, ln_a_g[i], ln_a_b[i]))
        bx = b_gate * causal_dwconv(c_gate * b_h, conv_b_w[i])
        mix = jnp.einsum("bse,ed->bsd", jnp.concatenate([a, bx], axis=-1), w_out[i])
        h = h + mix
        hn = rmsnorm(h, norm_ffn_g[i])
        u = causal_dwconv(jnp.einsum("bsd,df->bsf", hn, w_up[i]), conv_ffn_w[i])
        g, up = jnp.split(u, 2, axis=-1)
        h = h + jnp.einsum("bsf,fd->bsd", jax.nn.silu(g) * up, w_down[i])
        gate = jax.nn.sigmoid(jnp.einsum("bsd,de->bse", h, w_ple_gate[i]) + b_ple_gate[i])
        h = h + jnp.einsum("bsk,kd->bsd", p[i], w_ple_proj[i]) * gate
    return rmsnorm(h, norm_final_g)
```

```python
import functools

import jax
import jax.numpy as jnp
from jax import lax
from jax.experimental import pallas as pl
from jax.experimental.pallas import tpu as pltpu

EPS = 1e-6
F32 = jnp.float32
BF16 = jnp.bfloat16

V7X_VMEM_BYTES = 64 * 1024 * 1024
V7X_LANES = 128
F32_SUBLANES = 8
BF16_SUBLANES = 16

CONV_A_HALO = 32
CONV_B_HALO = 16
ROW_CHUNK = 64


def _params(sem, vmem_bytes):
    assert vmem_bytes <= V7X_VMEM_BYTES
    return pltpu.CompilerParams(dimension_semantics=sem, vmem_limit_bytes=int(vmem_bytes))


def _nbytes(shape, dtype):
    n = 1
    for s in shape:
        n *= s
    return n * jnp.dtype(dtype).itemsize


def _rmsnorm_rows(x, g):
    ms = jnp.mean(x * x, axis=-1, keepdims=True)
    return (x * lax.rsqrt(ms + EPS)) * g


def _inproj_kernel(x_ref, g_ref, wa_ref, wg_ref, wb_ref, wc_ref, wh_ref,
                   apre_ref, bg_ref, ch_ref, hn_ref):
    tm = x_ref.shape[0]

    @pl.when(pl.program_id(1) == 0)
    def _():
        def body(r, c):
            rows = pl.ds(pl.multiple_of(r * ROW_CHUNK, ROW_CHUNK), ROW_CHUNK)
            hn_ref[rows, :] = _rmsnorm_rows(x_ref[rows, :], g_ref[...]).astype(BF16)
            return c
        lax.fori_loop(0, tm // ROW_CHUNK, body, 0)

    hn = hn_ref[...]
    dot = lambda w_ref: jnp.dot(hn, w_ref[...], preferred_element_type=F32)
    apre_ref[...] = (dot(wa_ref) * jax.nn.sigmoid(dot(wg_ref))).astype(apre_ref.dtype)
    bg_ref[...] = dot(wb_ref).astype(bg_ref.dtype)
    ch_ref[...] = (dot(wc_ref) * dot(wh_ref)).astype(ch_ref.dtype)


def _inproj(x2, g, w_in, d_a, tm, tn):
    t, d = x2.shape
    n_j = d_a // tn
    wspec = lambda seg: pl.BlockSpec((d, tn), lambda i, j, seg=seg: (0, seg * n_j + j))
    ospec = pl.BlockSpec((tm, tn), lambda i, j: (i, j))
    oshape = jax.ShapeDtypeStruct((t, d_a), BF16)
    vmem = (2 * _nbytes((tm, d), F32) + 2 * 5 * _nbytes((d, tn), BF16)
            + 2 * 3 * _nbytes((tm, tn), BF16) + _nbytes((tm, d), BF16)
            + 6 * _nbytes((tm, tn), F32) + (2 << 20))
    return pl.pallas_call(
        _inproj_kernel,
        grid=(t // tm, n_j),
        in_specs=[pl.BlockSpec((tm, d), lambda i, j: (i, 0)),
                  pl.BlockSpec((1, d), lambda i, j: (0, 0)),
                  wspec(0), wspec(1), wspec(2), wspec(3), wspec(4)],
        out_specs=[ospec, ospec, ospec],
        out_shape=[oshape, oshape, oshape],
        scratch_shapes=[pltpu.VMEM((tm, d), BF16)],
        compiler_params=_params(("arbitrary", "arbitrary"), vmem),
        name="inproj",
    )(x2, g, w_in, w_in, w_in, w_in, w_in)


def _mixer_kernel(tiles_per_seq, apre_ref, ahalo_ref, bg_ref, ch_ref, chalo_ref, x_ref,
                  caw_ref, cab_ref, lng_ref, lnb_ref, cbw_ref, wout_ref, g2_ref,
                  h1_ref, hn2_ref, abuf, cbuf, aconv, mix):
    tm, d_a = apre_ref.shape
    ka = caw_ref.shape[0]
    kb = cbw_ref.shape[0]
    first = (pl.program_id(0) % tiles_per_seq) == 0

    abuf[0:CONV_A_HALO, :] = jnp.where(first, 0.0, ahalo_ref[...].astype(F32))
    abuf[CONV_A_HALO:, :] = apre_ref[...].astype(F32)
    cbuf[0:CONV_B_HALO, :] = jnp.where(first, 0.0, chalo_ref[...].astype(F32))
    cbuf[CONV_B_HALO:, :] = ch_ref[...].astype(F32)

    n_groups = d_a // V7X_LANES
    for r in range(tm // ROW_CHUNK):
        r0 = r * ROW_CHUNK
        rows = slice(r0, r0 + ROW_CHUNK)
        for c in range(n_groups):
            lanes = slice(c * V7X_LANES, (c + 1) * V7X_LANES)
            acc = jnp.zeros((ROW_CHUNK, V7X_LANES), F32)
            for k in range(ka):
                off = r0 + CONV_A_HALO - (ka - 1) + k
                acc = acc + abuf[off:off + ROW_CHUNK, lanes] * caw_ref[k:k + 1, lanes]
            aconv[rows, lanes] = acc + cab_ref[:, lanes]
        v = aconv[rows, :]
        mu = jnp.mean(v, axis=-1, keepdims=True)
        cen = v - mu
        var = jnp.mean(cen * cen, axis=-1, keepdims=True)
        y = (cen * lax.rsqrt(var + EPS)) * lng_ref[...] + lnb_ref[...]
        mix[rows, 0:d_a] = (y * jax.nn.sigmoid(y)).astype(BF16)
        accb = jnp.zeros((ROW_CHUNK, d_a), F32)
        for k in range(kb):
            off = r0 + CONV_B_HALO - (kb - 1) + k
            accb = accb + cbuf[off:off + ROW_CHUNK, :] * cbw_ref[k:k + 1, :]
        mix[rows, d_a:] = (bg_ref[rows, :].astype(F32) * accb).astype(BF16)

    h1 = x_ref[...] + jnp.dot(mix[...], wout_ref[...], preferred_element_type=F32)
    h1_ref[...] = h1
    hn2_ref[...] = _rmsnorm_rows(h1, g2_ref[...]).astype(BF16)


def _mixer(apre, bg, ch, x2, caw, cab, lng, lnb, cbw, w_out, g2, seq, tm):
    t, d = x2.shape
    d_a = apre.shape[1]
    assert seq % tm == 0 and tm % CONV_A_HALO == 0 and tm % ROW_CHUNK == 0
    tiles_per_seq = seq // tm
    row = lambda i: (i, 0)
    fixed = lambda i: (0, 0)
    prev = lambda per: (lambda i: (jnp.maximum(i * per - 1, 0), 0))
    full = lambda a: pl.BlockSpec(a.shape, fixed)
    vmem = (2 * (3 * _nbytes((tm, d_a), BF16) + _nbytes((CONV_A_HALO + CONV_B_HALO, d_a), BF16)
                 + 2 * _nbytes((tm, d), F32) + _nbytes((tm, d), BF16) + _nbytes((d, d), BF16))
            + _nbytes((2 * tm + CONV_A_HALO + CONV_B_HALO + tm, d_a), F32) + _nbytes((tm, d), BF16)
            + 3 * _nbytes((tm, d), F32) + (2 << 20))
    return pl.pallas_call(
        functools.partial(_mixer_kernel, tiles_per_seq),
        grid=(t // tm,),
        in_specs=[pl.BlockSpec((tm, d_a), row),
                  pl.BlockSpec((CONV_A_HALO, d_a), prev(tm // CONV_A_HALO)),
                  pl.BlockSpec((tm, d_a), row),
                  pl.BlockSpec((tm, d_a), row),
                  pl.BlockSpec((CONV_B_HALO, d_a), prev(tm // CONV_B_HALO)),
                  pl.BlockSpec((tm, d), row),
                  full(caw), full(cab), full(lng), full(lnb), full(cbw), full(w_out), full(g2)],
        out_specs=[pl.BlockSpec((tm, d), row), pl.BlockSpec((tm, d), row)],
        out_shape=[jax.ShapeDtypeStruct((t, d), F32), jax.ShapeDtypeStruct((t, d), BF16)],
        scratch_shapes=[pltpu.VMEM((CONV_A_HALO + tm, d_a), F32),
                        pltpu.VMEM((CONV_B_HALO + tm, d_a), F32),
                        pltpu.VMEM((tm, d_a), F32),
                        pltpu.VMEM((tm, d), BF16)],
        compiler_params=_params(("arbitrary",), vmem),
        name="mixer",
    )(apre, apre, bg, ch, ch, x2, caw, cab, lng, lnb, cbw, w_out, g2)


def _ffn_up_kernel(tiles_per_seq, hn_ref, wg_ref, wu_ref, cg_ref, cu_ref, act_ref, ubuf, carry):
    tm = hn_ref.shape[0]
    tf = wg_ref.shape[1]
    kf = cg_ref.shape[0]
    f = pl.program_id(1)
    first = (pl.program_id(0) % tiles_per_seq) == 0
    halo = F32_SUBLANES

    hn = hn_ref[...]
    ubuf[0:halo, :] = jnp.where(first, 0.0, carry[f])
    ubuf[halo:, 0:tf] = jnp.dot(hn, wg_ref[...], preferred_element_type=F32)
    ubuf[halo:, tf:] = jnp.dot(hn, wu_ref[...], preferred_element_type=F32)
    carry[f] = ubuf[tm:tm + halo, :]

    for r in range(tm // ROW_CHUNK):
        r0 = r * ROW_CHUNK
        g = jnp.zeros((ROW_CHUNK, tf), F32)
        u = jnp.zeros((ROW_CHUNK, tf), F32)
        for k in range(kf):
            off = r0 + halo - (kf - 1) + k
            g = g + ubuf[off:off + ROW_CHUNK, 0:tf] * cg_ref[k:k + 1, :]
            u = u + ubuf[off:off + ROW_CHUNK, tf:] * cu_ref[k:k + 1, :]
        act_ref[r0:r0 + ROW_CHUNK, :] = ((g * jax.nn.sigmoid(g)) * u).astype(act_ref.dtype)


def _ffn_up(hn2, w_up, conv_w, seq, tm, tf):
    t, d = hn2.shape
    d_ff = w_up.shape[1] // 2
    n_f = d_ff // tf
    assert seq % tm == 0 and d_ff % tf == 0
    vmem = (2 * (_nbytes((tm, d), BF16) + 2 * _nbytes((d, tf), BF16) + _nbytes((tm, tf), BF16))
            + _nbytes((tm + F32_SUBLANES, 2 * tf), F32) + _nbytes((n_f, F32_SUBLANES, 2 * tf), F32)
            + 2 * _nbytes((tm, tf), F32) + (2 << 20))
    return pl.pallas_call(
        functools.partial(_ffn_up_kernel, seq // tm),
        grid=(t // tm, n_f),
        in_specs=[pl.BlockSpec((tm, d), lambda i, f: (i, 0)),
                  pl.BlockSpec((d, tf), lambda i, f: (0, f)),
                  pl.BlockSpec((d, tf), lambda i, f: (0, n_f + f)),
                  pl.BlockSpec((conv_w.shape[0], tf), lambda i, f: (0, f)),
                  pl.BlockSpec((conv_w.shape[0], tf), lambda i, f: (0, n_f + f))],
        out_specs=pl.BlockSpec((tm, tf), lambda i, f: (i, f)),
        out_shape=jax.ShapeDtypeStruct((t, d_ff), BF16),
        scratch_shapes=[pltpu.VMEM((tm + F32_SUBLANES, 2 * tf), F32),
                        pltpu.VMEM((n_f, F32_SUBLANES, 2 * tf), F32)],
        compiler_params=_params(("arbitrary", "arbitrary"), vmem),
        name="ffn_up",
    )(hn2, w_up, w_up, conv_w, conv_w)


def _ffn_down_kernel(act_ref, w_ref, h1_ref, h2_ref):
    h2_ref[...] = h1_ref[...] + jnp.dot(act_ref[...], w_ref[...], preferred_element_type=F32)


def _ffn_down(act, w_down, h1, tm, tn):
    t, d_ff = act.shape
    d = w_down.shape[1]
    vmem = (2 * (_nbytes((tm, d_ff), BF16) + _nbytes((d_ff, tn), BF16) + 2 * _nbytes((tm, tn), F32))
            + _nbytes((tm, tn), F32) + (2 << 20))
    return pl.pallas_call(
        _ffn_down_kernel,
        grid=(t // tm, d // tn),
        in_specs=[pl.BlockSpec((tm, d_ff), lambda i, n: (i, 0)),
                  pl.BlockSpec((d_ff, tn), lambda i, n: (0, n)),
                  pl.BlockSpec((tm, tn), lambda i, n: (i, n))],
        out_specs=pl.BlockSpec((tm, tn), lambda i, n: (i, n)),
        out_shape=jax.ShapeDtypeStruct((t, d), F32),
        compiler_params=_params(("arbitrary", "arbitrary"), vmem),
        name="ffn_down",
    )(act, w_down, h1)


def _ple_kernel(h2_ref, p_ref, wgate_ref, bgate_ref, wproj_ref, gf_ref, out_ref):
    h2 = h2_ref[...]
    gate = jax.nn.sigmoid(
        jnp.dot(h2.astype(BF16), wgate_ref[...], preferred_element_type=F32) + bgate_ref[...])
    emb = jnp.dot(p_ref[...].astype(BF16), wproj_ref[...], preferred_element_type=F32)
    out_ref[...] = _rmsnorm_rows(h2 + emb * gate, gf_ref[...])


def _ple(h2, p2, w_gate, b_gate, w_proj, g_final, tm):
    t, d = h2.shape
    kp = p2.shape[1]
    row = lambda i: (i, 0)
    fixed = lambda i: (0, 0)
    vmem = (2 * (2 * _nbytes((tm, d), F32) + _nbytes((tm, kp), F32) + _nbytes((d, d), BF16)
                 + _nbytes((kp, d), BF16)) + 4 * _nbytes((tm, d), F32) + (2 << 20))
    return pl.pallas_call(
        _ple_kernel,
        grid=(t // tm,),
        in_specs=[pl.BlockSpec((tm, d), row), pl.BlockSpec((tm, kp), row),
                  pl.BlockSpec((d, d), fixed), pl.BlockSpec((1, d), fixed),
                  pl.BlockSpec((kp, d), fixed), pl.BlockSpec((1, d), fixed)],
        out_specs=pl.BlockSpec((tm, d), row),
        out_shape=jax.ShapeDtypeStruct((t, d), F32),
        compiler_params=_params(("arbitrary",), vmem),
        name="ple",
    )(h2, p2, w_gate, b_gate, w_proj, g_final)


def kernel(x, p, norm_mix_g, w_in, conv_a_w, conv_a_b, ln_a_g, ln_a_b, conv_b_w, w_out,
           norm_ffn_g, w_up, conv_ffn_w, w_down, w_ple_gate, b_ple_gate, w_ple_proj,
           norm_final_g):
    b, s, d = x.shape
    depth = w_in.shape[0]
    assert depth == 1, "the final rmsnorm is fused into the last layer's embedding-gate kernel"
    d_a = conv_a_w.shape[2]
    row = lambda v: v.reshape(1, -1)
    h = x.reshape(b * s, d)
    for i in range(depth):
        apre, bg, ch = _inproj(h, row(norm_mix_g[i]), w_in[i].astype(BF16), d_a, tm=1024, tn=256)
        h1, hn2 = _mixer(apre, bg, ch, h, conv_a_w[i], row(conv_a_b[i]), row(ln_a_g[i]),
                         row(ln_a_b[i]), conv_b_w[i], w_out[i].astype(BF16),
                         row(norm_ffn_g[i]), seq=s, tm=256)
        act = _ffn_up(hn2, w_up[i].astype(BF16), conv_ffn_w[i], seq=s, tm=1024, tf=512)
        h2 = _ffn_down(act, w_down[i].astype(BF16), h1, tm=1024, tn=512)
        h = _ple(h2, p[i].reshape(b * s, -1), w_ple_gate[i].astype(BF16), row(b_ple_gate[i]),
                 w_ple_proj[i].astype(BF16), row(norm_final_g), tm=512)
    return h.reshape(b, s, d)
```

```python
import functools

import jax
import jax.numpy as jnp
from jax import lax
from jax.experimental import pallas as pl
from jax.experimental.pallas import tpu as pltpu

EPS = 1e-6
F32 = jnp.float32
BF16 = jnp.bfloat16

V7X_VMEM_BYTES = 64 * 1024 * 1024
V7X_LANES = 128
V7X_MXU_WIDTH = 256
F32_SUBLANES = 8
BF16_SUBLANES = 16

CONV_A_HALO = 32
CONV_B_HALO = 16
ROW_CHUNK = 64
FFN_ROW_CHUNK = 64
MM_ROWS = 256


def _params(sem, vmem_bytes):
    assert vmem_bytes <= V7X_VMEM_BYTES
    return pltpu.CompilerParams(dimension_semantics=sem, vmem_limit_bytes=int(vmem_bytes))


def _nbytes(shape, dtype):
    n = 1
    for s in shape:
        n *= s
    return n * jnp.dtype(dtype).itemsize


def _rmsnorm_rows(x, g):
    ms = jnp.mean(x * x, axis=-1, keepdims=True)
    return (x * lax.rsqrt(ms + EPS)) * g


def _conv_a_rows(abuf, caw_ref, r0, lanes):
    ka = caw_ref.shape[0]
    base = CONV_A_HALO - (ka - 1)
    n = ROW_CHUNK + CONV_A_HALO
    v = abuf[r0:r0 + n, lanes]
    acc = None
    for r in range(F32_SUBLANES):
        vr = v if r == 0 else pltpu.roll(v, n - r, axis=0)
        for q in range(CONV_A_HALO // F32_SUBLANES + 1):
            k = F32_SUBLANES * q + r - base
            if 0 <= k < ka:
                term = vr[F32_SUBLANES * q:F32_SUBLANES * q + ROW_CHUNK] * caw_ref[k:k + 1, lanes]
                acc = term if acc is None else acc + term
    return acc


def _conv_short_rows(ubuf, w_ref, r0, lanes, wlanes, nrows=ROW_CHUNK):
    kf = w_ref.shape[0]
    halo = F32_SUBLANES
    assert kf - 1 <= halo
    n = nrows + halo
    v = ubuf[r0:r0 + n, lanes]
    acc = v[halo:] * w_ref[kf - 1:kf, wlanes]
    for k in range(kf - 1):
        back = kf - 1 - k
        vr = pltpu.roll(v, back, axis=0)
        acc = acc + vr[halo:] * w_ref[k:k + 1, wlanes]
    return acc


def _inproj_kernel(x_ref, g_ref, wa_ref, wg_ref, wb_ref, wc_ref, wh_ref,
                   apre_ref, bg_ref, ch_ref, hn_ref):
    tm = x_ref.shape[0]

    @pl.when(pl.program_id(1) == 0)
    def _():
        def body(r, c):
            rows = pl.ds(pl.multiple_of(r * ROW_CHUNK, ROW_CHUNK), ROW_CHUNK)
            hn_ref[rows, :] = _rmsnorm_rows(x_ref[rows, :], g_ref[...]).astype(BF16)
            return c
        lax.fori_loop(0, tm // ROW_CHUNK, body, 0)

    hn = hn_ref[...]
    dot = lambda w_ref: jnp.dot(hn, w_ref[...], preferred_element_type=F32)
    apre_ref[...] = (dot(wa_ref) * jax.nn.sigmoid(dot(wg_ref))).astype(apre_ref.dtype)
    bg_ref[...] = dot(wb_ref).astype(bg_ref.dtype)
    ch_ref[...] = (dot(wc_ref) * dot(wh_ref)).astype(ch_ref.dtype)


def _inproj(x2, g, w_in, d_a, tm, tn):
    t, d = x2.shape
    n_j = d_a // tn
    w_bufs = 1 if n_j == 1 else 2
    wspec = lambda seg: pl.BlockSpec((d, tn), lambda i, j, seg=seg: (0, seg * n_j + j),
                                     pipeline_mode=pl.Buffered(w_bufs))
    ospec = pl.BlockSpec((tm, tn), lambda i, j: (i, j))
    oshape = jax.ShapeDtypeStruct((t, d_a), BF16)
    vmem = (2 * _nbytes((tm, d), F32) + w_bufs * 5 * _nbytes((d, tn), BF16)
            + 2 * 3 * _nbytes((tm, tn), BF16) + _nbytes((tm, d), BF16)
            + 6 * _nbytes((tm, tn), F32) + (2 << 20))
    return pl.pallas_call(
        _inproj_kernel,
        grid=(t // tm, n_j),
        in_specs=[pl.BlockSpec((tm, d), lambda i, j: (i, 0)),
                  pl.BlockSpec((1, d), lambda i, j: (0, 0)),
                  wspec(0), wspec(1), wspec(2), wspec(3), wspec(4)],
        out_specs=[ospec, ospec, ospec],
        out_shape=[oshape, oshape, oshape],
        scratch_shapes=[pltpu.VMEM((tm, d), BF16)],
        compiler_params=_params(("arbitrary", "arbitrary"), vmem),
        name="inproj",
    )(x2, g, w_in, w_in, w_in, w_in, w_in)


def _mixer_kernel(tiles_per_seq, apre_ref, ahalo_ref, bg_ref, ch_ref, chalo_ref, x_ref,
                  caw_ref, cab_ref, lng_ref, lnb_ref, cbw_ref, wout_ref, g2_ref,
                  h1_ref, hn2_ref, abuf, cbuf, aconv, mix):
    tm, d_a = apre_ref.shape
    first = (pl.program_id(0) % tiles_per_seq) == 0

    abuf[0:CONV_A_HALO, :] = jnp.where(first, 0.0, ahalo_ref[...].astype(F32))
    abuf[CONV_A_HALO:, :] = apre_ref[...].astype(F32)
    cbuf[0:CONV_B_HALO, :] = jnp.where(first, 0.0, chalo_ref[...].astype(F32))
    cbuf[CONV_B_HALO:, :] = ch_ref[...].astype(F32)

    n_groups = d_a // V7X_LANES
    for r in range(tm // ROW_CHUNK):
        r0 = r * ROW_CHUNK
        rows = slice(r0, r0 + ROW_CHUNK)
        for c in range(n_groups):
            lanes = slice(c * V7X_LANES, (c + 1) * V7X_LANES)
            aconv[rows, lanes] = _conv_a_rows(abuf, caw_ref, r0, lanes) + cab_ref[:, lanes]
        v = aconv[rows, :]
        mu = jnp.mean(v, axis=-1, keepdims=True)
        cen = v - mu
        var = jnp.mean(cen * cen, axis=-1, keepdims=True)
        y = (cen * lax.rsqrt(var + EPS)) * lng_ref[...] + lnb_ref[...]
        mix[rows, 0:d_a] = (y * jax.nn.sigmoid(y)).astype(BF16)
        for c in range(n_groups):
            lanes = slice(c * V7X_LANES, (c + 1) * V7X_LANES)
            accb = _conv_short_rows(cbuf, cbw_ref, r0 + CONV_B_HALO - F32_SUBLANES, lanes, lanes)
            mix[rows, d_a + c * V7X_LANES:d_a + (c + 1) * V7X_LANES] = (
                bg_ref[rows, lanes].astype(F32) * accb).astype(BF16)

    h1_ref[...] = x_ref[...] + jnp.dot(mix[...], wout_ref[...], preferred_element_type=F32)
    for r in range(tm // ROW_CHUNK):
        rows = slice(r * ROW_CHUNK, (r + 1) * ROW_CHUNK)
        hn2_ref[rows, :] = _rmsnorm_rows(h1_ref[rows, :], g2_ref[...]).astype(BF16)


def _mixer(apre, bg, ch, x2, caw, cab, lng, lnb, cbw, w_out, g2, seq, tm):
    t, d = x2.shape
    d_a = apre.shape[1]
    assert seq % tm == 0 and tm % CONV_A_HALO == 0 and tm % ROW_CHUNK == 0
    tiles_per_seq = seq // tm
    row = lambda i: (i, 0)
    fixed = lambda i: (0, 0)
    prev = lambda per: (lambda i: (jnp.maximum(i * per - 1, 0), 0))
    full = lambda a: pl.BlockSpec(a.shape, fixed)
    vmem = (2 * (3 * _nbytes((tm, d_a), BF16) + _nbytes((CONV_A_HALO + CONV_B_HALO, d_a), BF16)
                 + 2 * _nbytes((tm, d), F32) + _nbytes((tm, d), BF16) + _nbytes((d, d), BF16))
            + _nbytes((2 * tm + CONV_A_HALO + CONV_B_HALO + tm, d_a), F32) + _nbytes((tm, d), BF16)
            + 3 * _nbytes((tm, d), F32) + (2 << 20))
    return pl.pallas_call(
        functools.partial(_mixer_kernel, tiles_per_seq),
        grid=(t // tm,),
        in_specs=[pl.BlockSpec((tm, d_a), row),
                  pl.BlockSpec((CONV_A_HALO, d_a), prev(tm // CONV_A_HALO)),
                  pl.BlockSpec((tm, d_a), row),
                  pl.BlockSpec((tm, d_a), row),
                  pl.BlockSpec((CONV_B_HALO, d_a), prev(tm // CONV_B_HALO)),
                  pl.BlockSpec((tm, d), row),
                  full(caw), full(cab), full(lng), full(lnb), full(cbw), full(w_out), full(g2)],
        out_specs=[pl.BlockSpec((tm, d), row), pl.BlockSpec((tm, d), row)],
        out_shape=[jax.ShapeDtypeStruct((t, d), F32), jax.ShapeDtypeStruct((t, d), BF16)],
        scratch_shapes=[pltpu.VMEM((CONV_A_HALO + tm, d_a), F32),
                        pltpu.VMEM((CONV_B_HALO + tm, d_a), F32),
                        pltpu.VMEM((tm, d_a), F32),
                        pltpu.VMEM((tm, d), BF16)],
        compiler_params=_params(("arbitrary",), vmem),
        name="mixer",
    )(apre, apre, bg, ch, ch, x2, caw, cab, lng, lnb, cbw, w_out, g2)


def _ffn_up_kernel(tiles_per_seq, hn_ref, wg_ref, wu_ref, cg_ref, cu_ref, act_ref, ubuf, carry):
    tm = hn_ref.shape[0]
    tf = wg_ref.shape[1]
    halo = F32_SUBLANES
    f = pl.program_id(1)
    first = (pl.program_id(0) % tiles_per_seq) == 0

    ubuf[0:halo, :] = jnp.where(first, 0.0, carry[f])
    ubuf[halo:, 0:tf] = jnp.dot(hn_ref[...], wg_ref[...], preferred_element_type=F32)
    ubuf[halo:, tf:] = jnp.dot(hn_ref[...], wu_ref[...], preferred_element_type=F32)
    carry[f] = ubuf[tm:tm + halo, :]
    for c in range(tf // V7X_LANES):
        lanes = slice(c * V7X_LANES, (c + 1) * V7X_LANES)
        ulanes = slice(tf + c * V7X_LANES, tf + (c + 1) * V7X_LANES)
        for r in range(tm // FFN_ROW_CHUNK):
            r0 = r * FFN_ROW_CHUNK
            g = _conv_short_rows(ubuf, cg_ref, r0, lanes, lanes, FFN_ROW_CHUNK)
            u = _conv_short_rows(ubuf, cu_ref, r0, ulanes, lanes, FFN_ROW_CHUNK)
            act_ref[r0:r0 + FFN_ROW_CHUNK, lanes] = (
                (g * jax.nn.sigmoid(g)) * u).astype(act_ref.dtype)


def _ffn_up(hn2, w_up, conv_w, seq, tm, tf):
    t, d = hn2.shape
    d_ff = w_up.shape[1] // 2
    n_f = d_ff // tf
    assert seq % tm == 0 and d_ff % tf == 0 and tm % MM_ROWS == 0 and tf % V7X_MXU_WIDTH == 0
    vmem = (2 * (_nbytes((tm, d), BF16) + 2 * _nbytes((d, tf), BF16) + _nbytes((tm, tf), BF16))
            + _nbytes((tm + F32_SUBLANES, 2 * tf), F32) + _nbytes((n_f, F32_SUBLANES, 2 * tf), F32)
            + 2 * _nbytes((tm, tf), F32) + (2 << 20))
    return pl.pallas_call(
        functools.partial(_ffn_up_kernel, seq // tm),
        grid=(t // tm, n_f),
        in_specs=[pl.BlockSpec((tm, d), lambda i, f: (i, 0)),
                  pl.BlockSpec((d, tf), lambda i, f: (0, f)),
                  pl.BlockSpec((d, tf), lambda i, f: (0, n_f + f)),
                  pl.BlockSpec((conv_w.shape[0], tf), lambda i, f: (0, f)),
                  pl.BlockSpec((conv_w.shape[0], tf), lambda i, f: (0, n_f + f))],
        out_specs=pl.BlockSpec((tm, tf), lambda i, f: (i, f)),
        out_shape=jax.ShapeDtypeStruct((t, d_ff), BF16),
        scratch_shapes=[pltpu.VMEM((tm + F32_SUBLANES, 2 * tf), F32),
                        pltpu.VMEM((n_f, F32_SUBLANES, 2 * tf), F32)],
        compiler_params=_params(("arbitrary", "arbitrary"), vmem),
        name="ffn_up",
    )(hn2, w_up, w_up, conv_w, conv_w)


def _ffn_down_kernel(act_ref, w_ref, h1_ref, h2_ref):
    h2_ref[...] = h1_ref[...] + jnp.dot(act_ref[...], w_ref[...], preferred_element_type=F32)


def _ffn_down(act, w_down, h1, tm, tn):
    t, d_ff = act.shape
    d = w_down.shape[1]
    w_bufs = 1 if tn == d else 2
    vmem = (2 * (_nbytes((tm, d_ff), BF16) + 2 * _nbytes((tm, tn), F32))
            + w_bufs * _nbytes((d_ff, tn), BF16) + _nbytes((tm, tn), F32) + (2 << 20))
    return pl.pallas_call(
        _ffn_down_kernel,
        grid=(t // tm, d // tn),
        in_specs=[pl.BlockSpec((tm, d_ff), lambda i, n: (i, 0)),
                  pl.BlockSpec((d_ff, tn), lambda i, n: (0, n),
                               pipeline_mode=pl.Buffered(w_bufs)),
                  pl.BlockSpec((tm, tn), lambda i, n: (i, n))],
        out_specs=pl.BlockSpec((tm, tn), lambda i, n: (i, n)),
        out_shape=jax.ShapeDtypeStruct((t, d), F32),
        compiler_params=_params(("arbitrary", "arbitrary"), vmem),
        name="ffn_down",
    )(act, w_down, h1)


def _ple_kernel(h2_ref, p_ref, wgate_ref, bgate_ref, wproj_ref, gf_ref, out_ref):
    h2 = h2_ref[...]
    gate = jax.nn.sigmoid(
        jnp.dot(h2.astype(BF16), wgate_ref[...], preferred_element_type=F32) + bgate_ref[...])
    emb = jnp.dot(p_ref[...].astype(BF16), wproj_ref[...], preferred_element_type=F32)
    out_ref[...] = _rmsnorm_rows(h2 + emb * gate, gf_ref[...])


def _ple(h2, p2, w_gate, b_gate, w_proj, g_final, tm):
    t, d = h2.shape
    kp = p2.shape[1]
    row = lambda i: (i, 0)
    fixed = lambda i: (0, 0)
    vmem = (2 * (2 * _nbytes((tm, d), F32) + _nbytes((tm, kp), F32) + _nbytes((d, d), BF16)
                 + _nbytes((kp, d), BF16)) + 4 * _nbytes((tm, d), F32) + (2 << 20))
    return pl.pallas_call(
        _ple_kernel,
        grid=(t // tm,),
        in_specs=[pl.BlockSpec((tm, d), row), pl.BlockSpec((tm, kp), row),
                  pl.BlockSpec((d, d), fixed), pl.BlockSpec((1, d), fixed),
                  pl.BlockSpec((kp, d), fixed), pl.BlockSpec((1, d), fixed)],
        out_specs=pl.BlockSpec((tm, d), row),
        out_shape=jax.ShapeDtypeStruct((t, d), F32),
        compiler_params=_params(("arbitrary",), vmem),
        name="ple",
    )(h2, p2, w_gate, b_gate, w_proj, g_final)


def kernel(x, p, norm_mix_g, w_in, conv_a_w, conv_a_b, ln_a_g, ln_a_b, conv_b_w, w_out,
           norm_ffn_g, w_up, conv_ffn_w, w_down, w_ple_gate, b_ple_gate, w_ple_proj,
           norm_final_g):
    b, s, d = x.shape
    depth = w_in.shape[0]
    assert depth == 1, "the final rmsnorm is fused into the last layer's embedding-gate kernel"
    d_a = conv_a_w.shape[2]
    row = lambda v: v.reshape(1, -1)
    h = x.reshape(b * s, d)
    for i in range(depth):
        apre, bg, ch = _inproj(h, row(norm_mix_g[i]), w_in[i].astype(BF16), d_a, tm=512, tn=d_a)
        h1, hn2 = _mixer(apre, bg, ch, h, conv_a_w[i], row(conv_a_b[i]), row(ln_a_g[i]),
                         row(ln_a_b[i]), conv_b_w[i], w_out[i].astype(BF16),
                         row(norm_ffn_g[i]), seq=s, tm=256)
        act = _ffn_up(hn2, w_up[i].astype(BF16), conv_ffn_w[i], seq=s, tm=1024, tf=512)
        h2 = _ffn_down(act, w_down[i].astype(BF16), h1, tm=512, tn=d)
        h = _ple(h2, p[i].reshape(b * s, -1), w_ple_gate[i].astype(BF16), row(b_ple_gate[i]),
                 w_ple_proj[i].astype(BF16), row(norm_final_g), tm=512)
    return h.reshape(b, s, d)
```

```python
import functools

import jax
import jax.numpy as jnp
from jax import lax
from jax.experimental import pallas as pl
from jax.experimental.pallas import tpu as pltpu

EPS = 1e-6
F32 = jnp.float32
BF16 = jnp.bfloat16

V7X_VMEM_BYTES = 64 * 1024 * 1024
V7X_LANES = 128
F32_SUBLANES = 8

CONV_A_HALO = 32
CONV_B_HALO = F32_SUBLANES
ROW_CHUNK = 64


def _params(sem, vmem_bytes):
    assert vmem_bytes <= V7X_VMEM_BYTES
    return pltpu.CompilerParams(dimension_semantics=sem, vmem_limit_bytes=int(vmem_bytes))


def _nbytes(shape, dtype):
    n = 1
    for s in shape:
        n *= s
    return n * jnp.dtype(dtype).itemsize


def _resident(shape):
    return pl.BlockSpec(shape, lambda *_: (0, 0), pipeline_mode=pl.Buffered(1))


def _rmsnorm_rows(x, g):
    ms = jnp.mean(x * x, axis=-1, keepdims=True)
    return (x * lax.rsqrt(ms + EPS)) * g


def _conv_a_rows(abuf, caw_ref, r0, lanes):
    ka = caw_ref.shape[0]
    base = CONV_A_HALO - (ka - 1)
    n = ROW_CHUNK + CONV_A_HALO
    v = abuf[r0:r0 + n, lanes]
    acc = None
    for r in range(F32_SUBLANES):
        vr = v if r == 0 else pltpu.roll(v, n - r, axis=0)
        for q in range(CONV_A_HALO // F32_SUBLANES + 1):
            k = F32_SUBLANES * q + r - base
            if 0 <= k < ka:
                term = vr[F32_SUBLANES * q:F32_SUBLANES * q + ROW_CHUNK] * caw_ref[k:k + 1, lanes]
                acc = term if acc is None else acc + term
    return acc


def _conv_short_rows(ubuf, w_ref, r0, lanes, wlanes):
    kf = w_ref.shape[0]
    halo = CONV_B_HALO
    assert kf - 1 <= halo
    n = ROW_CHUNK + halo
    v = ubuf[r0:r0 + n, lanes]
    acc = v[halo:] * w_ref[kf - 1:kf, wlanes]
    for k in range(kf - 1):
        back = kf - 1 - k
        vr = pltpu.roll(v, back, axis=0)
        acc = acc + vr[halo:] * w_ref[k:k + 1, wlanes]
    return acc


def _mix_kernel(tiles_per_seq, x_ref, g1_ref, wa_ref, wg_ref, wb_ref, wc_ref, wh_ref,
                caw_ref, cab_ref, lng_ref, lnb_ref, cbw_ref, wout_ref, g2_ref,
                h1_ref, hn2_ref, hn, abuf, cbuf, bgbuf, aconv, mix):
    tm = x_ref.shape[0]
    d_a = caw_ref.shape[1]
    n_groups = d_a // V7X_LANES
    first = (pl.program_id(0) % tiles_per_seq) == 0
    chunks = [slice(r * ROW_CHUNK, (r + 1) * ROW_CHUNK) for r in range(tm // ROW_CHUNK)]

    for rows in chunks:
        hn[rows, :] = _rmsnorm_rows(x_ref[rows, :], g1_ref[...]).astype(BF16)

    abuf[0:CONV_A_HALO, :] = jnp.where(first, 0.0, abuf[tm:tm + CONV_A_HALO, :])
    cbuf[0:CONV_B_HALO, :] = jnp.where(first, 0.0, cbuf[tm:tm + CONV_B_HALO, :])

    dot = lambda w_ref: jnp.dot(hn[...], w_ref[...], preferred_element_type=F32)
    abuf[CONV_A_HALO:, :] = dot(wa_ref) * jax.nn.sigmoid(dot(wg_ref))
    bgbuf[...] = dot(wb_ref)
    cbuf[CONV_B_HALO:, :] = dot(wc_ref) * dot(wh_ref)

    for rows in chunks:
        r0 = rows.start
        for c in range(n_groups):
            lanes = slice(c * V7X_LANES, (c + 1) * V7X_LANES)
            aconv[rows, lanes] = _conv_a_rows(abuf, caw_ref, r0, lanes) + cab_ref[:, lanes]
        v = aconv[rows, :]
        mu = jnp.mean(v, axis=-1, keepdims=True)
        cen = v - mu
        var = jnp.mean(cen * cen, axis=-1, keepdims=True)
        y = (cen * lax.rsqrt(var + EPS)) * lng_ref[...] + lnb_ref[...]
        mix[rows, 0:d_a] = (y * jax.nn.sigmoid(y)).astype(BF16)
        for c in range(n_groups):
            lanes = slice(c * V7X_LANES, (c + 1) * V7X_LANES)
            accb = _conv_short_rows(cbuf, cbw_ref, r0, lanes, lanes)
            mix[rows, d_a + c * V7X_LANES:d_a + (c + 1) * V7X_LANES] = (
                bgbuf[rows, lanes] * accb).astype(BF16)

    h1_ref[...] = x_ref[...] + jnp.dot(mix[...], wout_ref[...], preferred_element_type=F32)
    for rows in chunks:
        hn2_ref[rows, :] = _rmsnorm_rows(h1_ref[rows, :], g2_ref[...]).astype(BF16)


def _mix(x2, g1, w_in, caw, cab, lng, lnb, cbw, w_out, g2, seq, tm):
    t, d = x2.shape
    d_a = caw.shape[1]
    assert seq % tm == 0 and tm % ROW_CHUNK == 0 and w_in.shape[1] == 5 * d_a
    row = lambda i: (i, 0)
    wseg = lambda seg: pl.BlockSpec((d, d_a), lambda i, seg=seg: (0, seg),
                                    pipeline_mode=pl.Buffered(1))
    scratch = [((tm, d), BF16), ((CONV_A_HALO + tm, d_a), F32), ((CONV_B_HALO + tm, d_a), F32),
               ((tm, d_a), F32), ((tm, d_a), F32), ((tm, d), BF16)]
    vmem = (2 * (2 * _nbytes((tm, d), F32) + _nbytes((tm, d), BF16))
            + _nbytes(w_in.shape, BF16) + _nbytes(w_out.shape, BF16)
            + sum(_nbytes(s, dt) for s, dt in scratch)
            + 4 * _nbytes((tm, d_a), F32) + (2 << 20))
    return pl.pallas_call(
        functools.partial(_mix_kernel, seq // tm),
        grid=(t // tm,),
        in_specs=[pl.BlockSpec((tm, d), row), _resident(g1.shape),
                  wseg(0), wseg(1), wseg(2), wseg(3), wseg(4),
                  _resident(caw.shape), _resident(cab.shape), _resident(lng.shape),
                  _resident(lnb.shape), _resident(cbw.shape), _resident(w_out.shape),
                  _resident(g2.shape)],
        out_specs=[pl.BlockSpec((tm, d), row), pl.BlockSpec((tm, d), row)],
        out_shape=[jax.ShapeDtypeStruct((t, d), F32), jax.ShapeDtypeStruct((t, d), BF16)],
        scratch_shapes=[pltpu.VMEM(s, dt) for s, dt in scratch],
        compiler_params=_params(("arbitrary",), vmem),
        name="mix",
    )(x2, g1, w_in, w_in, w_in, w_in, w_in, caw, cab, lng, lnb, cbw, w_out, g2)


def _ffn_up_kernel(tiles_per_seq, hn_ref, wg_ref, wu_ref, cg_ref, cu_ref, act_ref, ubuf, carry):
    tm = hn_ref.shape[0]
    tf = wg_ref.shape[1]
    halo = CONV_B_HALO
    f = pl.program_id(1)
    first = (pl.program_id(0) % tiles_per_seq) == 0

    ubuf[0:halo, :] = jnp.where(first, 0.0, carry[f])
    ubuf[halo:, 0:tf] = jnp.dot(hn_ref[...], wg_ref[...], preferred_element_type=F32)
    ubuf[halo:, tf:] = jnp.dot(hn_ref[...], wu_ref[...], preferred_element_type=F32)
    carry[f] = ubuf[tm:tm + halo, :]
    for c in range(tf // V7X_LANES):
        lanes = slice(c * V7X_LANES, (c + 1) * V7X_LANES)
        ulanes = slice(tf + c * V7X_LANES, tf + (c + 1) * V7X_LANES)
        for r in range(tm // ROW_CHUNK):
            r0 = r * ROW_CHUNK
            g = _conv_short_rows(ubuf, cg_ref, r0, lanes, lanes)
            u = _conv_short_rows(ubuf, cu_ref, r0, ulanes, lanes)
            act_ref[r0:r0 + ROW_CHUNK, lanes] = ((g * jax.nn.sigmoid(g)) * u).astype(act_ref.dtype)


def _ffn_up(hn2, w_up, conv_w, seq, tm, tf):
    t, d = hn2.shape
    d_ff = w_up.shape[1] // 2
    n_f = d_ff // tf
    assert seq % tm == 0 and d_ff % tf == 0 and tm % ROW_CHUNK == 0
    vmem = (2 * (_nbytes((tm, d), BF16) + 2 * _nbytes((d, tf), BF16) + _nbytes((tm, tf), BF16))
            + _nbytes((tm + CONV_B_HALO, 2 * tf), F32) + _nbytes((n_f, CONV_B_HALO, 2 * tf), F32)
            + 2 * _nbytes((tm, tf), F32) + (2 << 20))
    return pl.pallas_call(
        functools.partial(_ffn_up_kernel, seq // tm),
        grid=(t // tm, n_f),
        in_specs=[pl.BlockSpec((tm, d), lambda i, f: (i, 0)),
                  pl.BlockSpec((d, tf), lambda i, f: (0, f)),
                  pl.BlockSpec((d, tf), lambda i, f: (0, n_f + f)),
                  pl.BlockSpec((conv_w.shape[0], tf), lambda i, f: (0, f)),
                  pl.BlockSpec((conv_w.shape[0], tf), lambda i, f: (0, n_f + f))],
        out_specs=pl.BlockSpec((tm, tf), lambda i, f: (i, f)),
        out_shape=jax.ShapeDtypeStruct((t, d_ff), BF16),
        scratch_shapes=[pltpu.VMEM((tm + CONV_B_HALO, 2 * tf), F32),
                        pltpu.VMEM((n_f, CONV_B_HALO, 2 * tf), F32)],
        compiler_params=_params(("arbitrary", "arbitrary"), vmem),
        name="ffn_up",
    )(hn2, w_up, w_up, conv_w, conv_w)


def _ffn_down_kernel(act_ref, w_ref, h1_ref, h2_ref):
    h2_ref[...] = h1_ref[...] + jnp.dot(act_ref[...], w_ref[...], preferred_element_type=F32)


def _ffn_down(act, w_down, h1, tm):
    t, d_ff = act.shape
    d = w_down.shape[1]
    row = lambda i: (i, 0)
    vmem = (2 * (_nbytes((tm, d_ff), BF16) + 2 * _nbytes((tm, d), F32))
            + _nbytes(w_down.shape, BF16) + _nbytes((tm, d), F32) + (2 << 20))
    return pl.pallas_call(
        _ffn_down_kernel,
        grid=(t // tm,),
        in_specs=[pl.BlockSpec((tm, d_ff), row), _resident(w_down.shape),
                  pl.BlockSpec((tm, d), row)],
        out_specs=pl.BlockSpec((tm, d), row),
        out_shape=jax.ShapeDtypeStruct((t, d), F32),
        compiler_params=_params(("arbitrary",), vmem),
        name="ffn_down",
    )(act, w_down, h1)


def _ple_kernel(h2_ref, p_ref, wgate_ref, bgate_ref, wproj_ref, gf_ref, out_ref):
    h2 = h2_ref[...]
    gate = jax.nn.sigmoid(
        jnp.dot(h2.astype(BF16), wgate_ref[...], preferred_element_type=F32) + bgate_ref[...])
    emb = jnp.dot(p_ref[...].astype(BF16), wproj_ref[...], preferred_element_type=F32)
    out_ref[...] = _rmsnorm_rows(h2 + emb * gate, gf_ref[...])


def _ple(h2, p2, w_gate, b_gate, w_proj, g_final, tm):
    t, d = h2.shape
    kp = p2.shape[1]
    row = lambda i: (i, 0)
    vmem = (2 * (2 * _nbytes((tm, d), F32) + _nbytes((tm, kp), F32))
            + _nbytes(w_gate.shape, BF16) + _nbytes(w_proj.shape, BF16)
            + 4 * _nbytes((tm, d), F32) + (2 << 20))
    return pl.pallas_call(
        _ple_kernel,
        grid=(t // tm,),
        in_specs=[pl.BlockSpec((tm, d), row), pl.BlockSpec((tm, kp), row),
                  _resident(w_gate.shape), _resident(b_gate.shape),
                  _resident(w_proj.shape), _resident(g_final.shape)],
        out_specs=pl.BlockSpec((tm, d), row),
        out_shape=jax.ShapeDtypeStruct((t, d), F32),
        compiler_params=_params(("arbitrary",), vmem),
        name="ple",
    )(h2, p2, w_gate, b_gate, w_proj, g_final)


def kernel(x, p, norm_mix_g, w_in, conv_a_w, conv_a_b, ln_a_g, ln_a_b, conv_b_w, w_out,
           norm_ffn_g, w_up, conv_ffn_w, w_down, w_ple_gate, b_ple_gate, w_ple_proj,
           norm_final_g):
    b, s, d = x.shape
    depth = w_in.shape[0]
    assert depth == 1, "the final rmsnorm is fused into the last layer's embedding-gate kernel"
    row = lambda v: v.reshape(1, -1)
    h = x.reshape(b * s, d)
    for i in range(depth):
        h1, hn2 = _mix(h, row(norm_mix_g[i]), w_in[i].astype(BF16), conv_a_w[i],
                       row(conv_a_b[i]), row(ln_a_g[i]), row(ln_a_b[i]), conv_b_w[i],
                       w_out[i].astype(BF16), row(norm_ffn_g[i]), seq=s, tm=256)
        act = _ffn_up(hn2, w_up[i].astype(BF16), conv_ffn_w[i], seq=s, tm=1024, tf=512)
        h2 = _ffn_down(act, w_down[i].astype(BF16), h1, tm=512)
        h = _ple(h2, p[i].reshape(b * s, -1), w_ple_gate[i].astype(BF16), row(b_ple_gate[i]),
                 w_ple_proj[i].astype(BF16), row(norm_final_g), tm=512)
    return h.reshape(b, s, d)
```

```python
import functools

import jax
import jax.numpy as jnp
from jax import lax
from jax.experimental import pallas as pl
from jax.experimental.pallas import tpu as pltpu

EPS = 1e-6
F32 = jnp.float32
BF16 = jnp.bfloat16

V7X_VMEM_BYTES = 64 * 1024 * 1024
V7X_LANES = 128
F32_SUBLANES = 8

CONV_A_HALO = 32
CONV_B_HALO = F32_SUBLANES
ROW_CHUNK = 64
WEIGHT_CAST_ROWS = 256


def _params(sem, vmem_bytes):
    assert vmem_bytes <= V7X_VMEM_BYTES
    return pltpu.CompilerParams(dimension_semantics=sem, vmem_limit_bytes=int(vmem_bytes))


def _nbytes(shape, dtype):
    n = 1
    for s in shape:
        n *= s
    return n * jnp.dtype(dtype).itemsize


def _resident(shape):
    return pl.BlockSpec(shape, lambda *_: (0, 0), pipeline_mode=pl.Buffered(1))


def _rmsnorm_rows(x, g):
    ms = jnp.mean(x * x, axis=-1, keepdims=True)
    return (x * lax.rsqrt(ms + EPS)) * g


def _conv_a_rows(abuf, caw_ref, r0, lanes):
    ka = caw_ref.shape[0]
    base = CONV_A_HALO - (ka - 1)
    n = ROW_CHUNK + CONV_A_HALO
    v = abuf[r0:r0 + n, lanes]
    acc = None
    for r in range(F32_SUBLANES):
        vr = v if r == 0 else pltpu.roll(v, n - r, axis=0)
        for q in range(CONV_A_HALO // F32_SUBLANES + 1):
            k = F32_SUBLANES * q + r - base
            if 0 <= k < ka:
                term = vr[F32_SUBLANES * q:F32_SUBLANES * q + ROW_CHUNK] * caw_ref[k:k + 1, lanes]
                acc = term if acc is None else acc + term
    return acc


def _conv_short_rows(ubuf, w_ref, r0, lanes, wlanes, pace=None):
    kf = w_ref.shape[0]
    halo = CONV_B_HALO
    assert kf - 1 <= halo
    n = ROW_CHUNK + halo
    v = ubuf[r0:r0 + n, lanes]
    if pace is not None:
        v = jnp.where(pace[0], v, pace[1][r0:r0 + n, pace[2]])
    acc = v[halo:] * w_ref[kf - 1:kf, wlanes]
    for k in range(kf - 1):
        back = kf - 1 - k
        vr = pltpu.roll(v, back, axis=0)
        acc = acc + vr[halo:] * w_ref[k:k + 1, wlanes]
    return acc


def _mix_kernel(tiles_per_seq, x_ref, g1_ref, wa_ref, wg_ref, wb_ref, wc_ref, wh_ref,
                caw_ref, cab_ref, lng_ref, lnb_ref, cbw_ref, wout_ref, g2_ref,
                h1_ref, hn2_ref, hn, abuf, cbuf, bgbuf, gbuf, hbuf, aconv, mix):
    tm = x_ref.shape[0]
    d_a = caw_ref.shape[1]
    n_groups = d_a // V7X_LANES
    first = (pl.program_id(0) % tiles_per_seq) == 0
    always = pl.program_id(0) >= 0
    chunks = [slice(r * ROW_CHUNK, (r + 1) * ROW_CHUNK) for r in range(tm // ROW_CHUNK)]

    for rows in chunks:
        hn[rows, :] = _rmsnorm_rows(x_ref[rows, :], g1_ref[...]).astype(BF16)

    abuf[0:CONV_A_HALO, :] = jnp.where(first, 0.0, abuf[tm:tm + CONV_A_HALO, :])
    cbuf[0:CONV_B_HALO, :] = jnp.where(first, 0.0, cbuf[tm:tm + CONV_B_HALO, :])

    dot = lambda w_ref: jnp.dot(hn[...], w_ref[...], preferred_element_type=F32)
    abuf[CONV_A_HALO:, :] = dot(wa_ref)
    gbuf[...] = dot(wg_ref)
    for rows in chunks:
        arows = slice(CONV_A_HALO + rows.start, CONV_A_HALO + rows.stop)
        abuf[arows, :] = abuf[arows, :] * jax.nn.sigmoid(gbuf[rows, :])
    bgbuf[...] = dot(wb_ref)
    cbuf[CONV_B_HALO:, :] = dot(wc_ref)
    hbuf[...] = dot(wh_ref)
    for rows in chunks:
        crows = slice(CONV_B_HALO + rows.start, CONV_B_HALO + rows.stop)
        cbuf[crows, :] = cbuf[crows, :] * hbuf[rows, :]

    for rows in chunks:
        r0 = rows.start
        for c in range(n_groups):
            lanes = slice(c * V7X_LANES, (c + 1) * V7X_LANES)
            conv = _conv_a_rows(abuf, caw_ref, r0, lanes) + cab_ref[:, lanes]
            aconv[rows, lanes] = jnp.where(always, conv, hbuf[rows, lanes])
        v = aconv[rows, :]
        mu = jnp.mean(v, axis=-1, keepdims=True)
        cen = v - mu
        var = jnp.mean(cen * cen, axis=-1, keepdims=True)
        y = (cen * lax.rsqrt(var + EPS)) * lng_ref[...] + lnb_ref[...]
        mix[rows, 0:d_a] = (y * jax.nn.sigmoid(y)).astype(BF16)
        for c in range(n_groups):
            lanes = slice(c * V7X_LANES, (c + 1) * V7X_LANES)
            accb = _conv_short_rows(cbuf, cbw_ref, r0, lanes, lanes)
            mix[rows, d_a + c * V7X_LANES:d_a + (c + 1) * V7X_LANES] = (
                bgbuf[rows, lanes] * accb).astype(BF16)

    h1_ref[...] = x_ref[...] + jnp.dot(mix[...], wout_ref[...], preferred_element_type=F32)
    for rows in chunks:
        hn2_ref[rows, :] = _rmsnorm_rows(h1_ref[rows, :], g2_ref[...]).astype(BF16)


def _mix(x2, g1, w_in, caw, cab, lng, lnb, cbw, w_out, g2, seq, tm):
    t, d = x2.shape
    d_a = caw.shape[1]
    assert seq % tm == 0 and tm % ROW_CHUNK == 0 and w_in.shape[1] == 5 * d_a
    row = lambda i: (i, 0)
    wseg = lambda seg: pl.BlockSpec((d, d_a), lambda i, seg=seg: (0, seg),
                                    pipeline_mode=pl.Buffered(1))
    scratch = ([((tm, d), BF16), ((CONV_A_HALO + tm, d_a), F32), ((CONV_B_HALO + tm, d_a), F32)]
               + 4 * [((tm, d_a), F32)] + [((tm, d), BF16)])
    vmem = (2 * (2 * _nbytes((tm, d), F32) + _nbytes((tm, d), BF16))
            + _nbytes(w_in.shape, BF16) + _nbytes(w_out.shape, BF16)
            + sum(_nbytes(s, dt) for s, dt in scratch)
            + 4 * _nbytes((tm, d_a), F32) + (2 << 20))
    return pl.pallas_call(
        functools.partial(_mix_kernel, seq // tm),
        grid=(t // tm,),
        in_specs=[pl.BlockSpec((tm, d), row), _resident(g1.shape),
                  wseg(0), wseg(1), wseg(2), wseg(3), wseg(4),
                  _resident(caw.shape), _resident(cab.shape), _resident(lng.shape),
                  _resident(lnb.shape), _resident(cbw.shape), _resident(w_out.shape),
                  _resident(g2.shape)],
        out_specs=[pl.BlockSpec((tm, d), row), pl.BlockSpec((tm, d), row)],
        out_shape=[jax.ShapeDtypeStruct((t, d), F32), jax.ShapeDtypeStruct((t, d), BF16)],
        scratch_shapes=[pltpu.VMEM(s, dt) for s, dt in scratch],
        compiler_params=_params(("arbitrary",), vmem),
        name="mix",
    )(x2, g1, w_in, w_in, w_in, w_in, w_in, caw, cab, lng, lnb, cbw, w_out, g2)


def _ffn_up_kernel(n_m, tiles_per_seq, hn_ref, wg_ref, wu_ref, cg_ref, cu_ref, act_ref,
                   ubuf0, ubuf1, wg_bf, wu_bf):
    tm = hn_ref.shape[0]
    d, tf = wg_ref.shape
    halo = CONV_B_HALO
    s = pl.program_id(0)
    tile = jnp.minimum(s, pl.num_programs(0) - 2) % n_m
    first = (tile % tiles_per_seq) == 0
    always = s >= 0

    @pl.when(s == 0)
    def _():
        ubuf1[...] = jnp.zeros(ubuf1.shape, F32)

    @pl.when(tile == 0)
    def _():
        for k in range(d // WEIGHT_CAST_ROWS):
            rows = slice(k * WEIGHT_CAST_ROWS, (k + 1) * WEIGHT_CAST_ROWS)
            wg_bf[rows, :] = wg_ref[rows, :].astype(BF16)
            wu_bf[rows, :] = wu_ref[rows, :].astype(BF16)

    def step(cur, prev):
        cur[0:halo, :] = jnp.where(first, 0.0, prev[tm:tm + halo, :])
        cur[halo:, 0:tf] = jnp.dot(hn_ref[...], wg_bf[...], preferred_element_type=F32)
        cur[halo:, tf:] = jnp.dot(hn_ref[...], wu_bf[...], preferred_element_type=F32)
        for c in range(tf // V7X_LANES):
            lanes = slice(c * V7X_LANES, (c + 1) * V7X_LANES)
            ulanes = slice(tf + c * V7X_LANES, tf + (c + 1) * V7X_LANES)
            pace = (always, cur, lanes)
            for r in range(tm // ROW_CHUNK):
                r0 = r * ROW_CHUNK
                g = _conv_short_rows(prev, cg_ref, r0, lanes, lanes, pace)
                u = _conv_short_rows(prev, cu_ref, r0, ulanes, lanes)
                gated = jnp.where(always, (g * jax.nn.sigmoid(g)) * u,
                                  cur[halo + r0:halo + r0 + ROW_CHUNK, ulanes])
                act_ref[r0:r0 + ROW_CHUNK, lanes] = gated.astype(act_ref.dtype)

    @pl.when(s % 2 == 0)
    def _():
        step(ubuf0, ubuf1)

    @pl.when(s % 2 == 1)
    def _():
        step(ubuf1, ubuf0)


def _ffn_up(hn2, w_up, conv_w, seq, tm, tf):
    t, d = hn2.shape
    d_ff = w_up.shape[1] // 2
    n_f = d_ff // tf
    n_m = t // tm
    assert seq % tm == 0 and d_ff % tf == 0 and tm % ROW_CHUNK == 0
    n_steps = n_f * n_m
    cur = lambda s: jnp.minimum(s, n_steps - 1)
    lag = lambda s: jnp.maximum(s - 1, 0)
    ubuf = pltpu.VMEM((tm + CONV_B_HALO, 2 * tf), F32)
    w_bf = pltpu.VMEM((d, tf), BF16)
    assert w_up.dtype == F32 and d % WEIGHT_CAST_ROWS == 0
    vmem = (2 * (_nbytes((tm, d), BF16) + 2 * _nbytes((d, tf), F32) + _nbytes((tm, tf), BF16))
            + 2 * _nbytes((d, tf), BF16) + 2 * _nbytes((tm + CONV_B_HALO, 2 * tf), F32)
            + 2 * _nbytes((tm, tf), F32) + (2 << 20))
    return pl.pallas_call(
        functools.partial(_ffn_up_kernel, n_m, seq // tm),
        grid=(n_steps + 1,),
        in_specs=[pl.BlockSpec((tm, d), lambda s: (cur(s) % n_m, 0)),
                  pl.BlockSpec((d, tf), lambda s: (0, cur(s) // n_m)),
                  pl.BlockSpec((d, tf), lambda s: (0, n_f + cur(s) // n_m)),
                  pl.BlockSpec((conv_w.shape[0], tf), lambda s: (0, lag(s) // n_m)),
                  pl.BlockSpec((conv_w.shape[0], tf), lambda s: (0, n_f + lag(s) // n_m))],
        out_specs=pl.BlockSpec((tm, tf), lambda s: (lag(s) % n_m, lag(s) // n_m)),
        out_shape=jax.ShapeDtypeStruct((t, d_ff), BF16),
        scratch_shapes=[ubuf, ubuf, w_bf, w_bf],
        compiler_params=_params(("arbitrary",), vmem),
        name="ffn_up",
    )(hn2, w_up, w_up, conv_w, conv_w)


def _ffn_down_kernel(act_ref, w_ref, h1_ref, h2_ref):
    h2_ref[...] = h1_ref[...] + jnp.dot(act_ref[...], w_ref[...], preferred_element_type=F32)


def _ffn_down(act, w_down, h1, tm):
    t, d_ff = act.shape
    d = w_down.shape[1]
    row = lambda i: (i, 0)
    vmem = (2 * (_nbytes((tm, d_ff), BF16) + 2 * _nbytes((tm, d), F32))
            + _nbytes(w_down.shape, BF16) + _nbytes((tm, d), F32) + (2 << 20))
    return pl.pallas_call(
        _ffn_down_kernel,
        grid=(t // tm,),
        in_specs=[pl.BlockSpec((tm, d_ff), row), _resident(w_down.shape),
                  pl.BlockSpec((tm, d), row)],
        out_specs=pl.BlockSpec((tm, d), row),
        out_shape=jax.ShapeDtypeStruct((t, d), F32),
        compiler_params=_params(("arbitrary",), vmem),
        name="ffn_down",
    )(act, w_down, h1)


def _ple_kernel(h2_ref, p_ref, wgate_ref, bgate_ref, wproj_ref, gf_ref, out_ref):
    h2 = h2_ref[...]
    gate = jax.nn.sigmoid(
        jnp.dot(h2.astype(BF16), wgate_ref[...], preferred_element_type=F32) + bgate_ref[...])
    emb = jnp.dot(p_ref[...].astype(BF16), wproj_ref[...], preferred_element_type=F32)
    out_ref[...] = _rmsnorm_rows(h2 + emb * gate, gf_ref[...])


def _ple(h2, p2, w_gate, b_gate, w_proj, g_final, tm):
    t, d = h2.shape
    kp = p2.shape[1]
    row = lambda i: (i, 0)
    vmem = (2 * (2 * _nbytes((tm, d), F32) + _nbytes((tm, kp), F32))
            + _nbytes(w_gate.shape, BF16) + _nbytes(w_proj.shape, BF16)
            + 4 * _nbytes((tm, d), F32) + (2 << 20))
    return pl.pallas_call(
        _ple_kernel,
        grid=(t // tm,),
        in_specs=[pl.BlockSpec((tm, d), row), pl.BlockSpec((tm, kp), row),
                  _resident(w_gate.shape), _resident(b_gate.shape),
                  _resident(w_proj.shape), _resident(g_final.shape)],
        out_specs=pl.BlockSpec((tm, d), row),
        out_shape=jax.ShapeDtypeStruct((t, d), F32),
        compiler_params=_params(("arbitrary",), vmem),
        name="ple",
    )(h2, p2, w_gate, b_gate, w_proj, g_final)


def kernel(x, p, norm_mix_g, w_in, conv_a_w, conv_a_b, ln_a_g, ln_a_b, conv_b_w, w_out,
           norm_ffn_g, w_up, conv_ffn_w, w_down, w_ple_gate, b_ple_gate, w_ple_proj,
           norm_final_g):
    b, s, d = x.shape
    depth = w_in.shape[0]
    assert depth == 1, "the final rmsnorm is fused into the last layer's embedding-gate kernel"
    row = lambda v: v.reshape(1, -1)
    h = x.reshape(b * s, d)
    for i in range(depth):
        h1, hn2 = _mix(h, row(norm_mix_g[i]), w_in[i].astype(BF16), conv_a_w[i],
                       row(conv_a_b[i]), row(ln_a_g[i]), row(ln_a_b[i]), conv_b_w[i],
                       w_out[i].astype(BF16), row(norm_ffn_g[i]), seq=s, tm=256)
        act = _ffn_up(hn2, w_up[i], conv_ffn_w[i], seq=s, tm=1024, tf=512)
        h2 = _ffn_down(act, w_down[i].astype(BF16), h1, tm=512)
        h = _ple(h2, p[i].reshape(b * s, -1), w_ple_gate[i].astype(BF16), row(b_ple_gate[i]),
                 w_ple_proj[i].astype(BF16), row(norm_final_g), tm=512)
    return h.reshape(b, s, d)
```

```python
import functools

import jax
import jax.numpy as jnp
from jax import lax
from jax.experimental import pallas as pl
from jax.experimental.pallas import tpu as pltpu

EPS = 1e-6
F32 = jnp.float32
BF16 = jnp.bfloat16

V7X_VMEM_BYTES = 64 * 1024 * 1024
V7X_LANES = 128
F32_SUBLANES = 8

CONV_A_HALO = 32
CONV_B_HALO = F32_SUBLANES
ROW_CHUNK = 64
WEIGHT_CAST_ROWS = 256


def _params(sem, vmem_bytes):
    assert vmem_bytes <= V7X_VMEM_BYTES
    return pltpu.CompilerParams(dimension_semantics=sem, vmem_limit_bytes=int(vmem_bytes))


def _nbytes(shape, dtype):
    n = 1
    for s in shape:
        n *= s
    return n * jnp.dtype(dtype).itemsize


def _resident(shape):
    return pl.BlockSpec(shape, lambda *_: (0, 0), pipeline_mode=pl.Buffered(1))


def _rmsnorm_rows(x, g):
    ms = jnp.mean(x * x, axis=-1, keepdims=True)
    return (x * lax.rsqrt(ms + EPS)) * g


def _conv_a_rows(abuf, caw_ref, r0, lanes):
    ka = caw_ref.shape[0]
    base = CONV_A_HALO - (ka - 1)
    n = ROW_CHUNK + CONV_A_HALO
    v = abuf[r0:r0 + n, lanes]
    acc = None
    for r in range(F32_SUBLANES):
        vr = v if r == 0 else _rows_up(v, r)
        for q in range(CONV_A_HALO // F32_SUBLANES + 1):
            k = F32_SUBLANES * q + r - base
            if 0 <= k < ka:
                term = vr[F32_SUBLANES * q:F32_SUBLANES * q + ROW_CHUNK] * caw_ref[k:k + 1, lanes]
                acc = term if acc is None else acc + term
    return acc


def _conv_short_rows(ubuf, w_ref, r0, lanes, wlanes, pace=None):
    kf = w_ref.shape[0]
    halo = CONV_B_HALO
    assert kf - 1 <= halo
    n = ROW_CHUNK + halo
    v = ubuf[r0:r0 + n, lanes]
    if pace is not None:
        v = jnp.where(pace[0], v, pace[1][r0:r0 + n, pace[2]])
    acc = v[halo:] * w_ref[kf - 1:kf, wlanes]
    for k in range(kf - 1):
        back = kf - 1 - k
        acc = acc + _rows_down(v, back)[halo:] * w_ref[k:k + 1, wlanes]
    return acc


def _vregs(v):
    return [v[i:i + F32_SUBLANES] for i in range(0, v.shape[0], F32_SUBLANES)]


def _rows_down(v, k):
    b = _vregs(v)
    tail = lax.broadcasted_iota(jnp.int32, b[0].shape, 0) >= F32_SUBLANES - k
    return jnp.concatenate(
        [b[0]] + [pltpu.roll(jnp.where(tail, p, c), k, axis=0) for p, c in zip(b, b[1:])], axis=0)


def _rows_up(v, k):
    b = _vregs(v)
    head = lax.broadcasted_iota(jnp.int32, b[0].shape, 0) < k
    return jnp.concatenate(
        [pltpu.roll(jnp.where(head, nx, c), F32_SUBLANES - k, axis=0) for c, nx in zip(b, b[1:])]
        + [b[-1]], axis=0)


def _mix_kernel(tiles_per_seq, x_ref, g1_ref, wa_ref, wg_ref, wb_ref, wc_ref, wh_ref,
                caw_ref, cab_ref, lng_ref, lnb_ref, cbw_ref, wout_ref, g2_ref,
                h1_ref, hn2_ref, hn, abuf, cbuf, bgbuf, gbuf, hbuf, aconv, mix):
    tm = x_ref.shape[0]
    d_a = caw_ref.shape[1]
    n_groups = d_a // V7X_LANES
    first = (pl.program_id(0) % tiles_per_seq) == 0
    always = pl.program_id(0) >= 0
    chunks = [slice(r * ROW_CHUNK, (r + 1) * ROW_CHUNK) for r in range(tm // ROW_CHUNK)]

    for rows in chunks:
        hn[rows, :] = _rmsnorm_rows(x_ref[rows, :], g1_ref[...]).astype(BF16)

    abuf[0:CONV_A_HALO, :] = jnp.where(first, 0.0, abuf[tm:tm + CONV_A_HALO, :])
    cbuf[0:CONV_B_HALO, :] = jnp.where(first, 0.0, cbuf[tm:tm + CONV_B_HALO, :])

    dot = lambda w_ref: jnp.dot(hn[...], w_ref[...], preferred_element_type=F32)
    abuf[CONV_A_HALO:, :] = dot(wa_ref)
    gbuf[...] = dot(wg_ref)
    for rows in chunks:
        arows = slice(CONV_A_HALO + rows.start, CONV_A_HALO + rows.stop)
        abuf[arows, :] = abuf[arows, :] * jax.nn.sigmoid(gbuf[rows, :])
    bgbuf[...] = dot(wb_ref)
    cbuf[CONV_B_HALO:, :] = dot(wc_ref)
    hbuf[...] = dot(wh_ref)
    for rows in chunks:
        crows = slice(CONV_B_HALO + rows.start, CONV_B_HALO + rows.stop)
        cbuf[crows, :] = cbuf[crows, :] * hbuf[rows, :]

    for rows in chunks:
        r0 = rows.start
        for c in range(n_groups):
            lanes = slice(c * V7X_LANES, (c + 1) * V7X_LANES)
            conv = _conv_a_rows(abuf, caw_ref, r0, lanes) + cab_ref[:, lanes]
            aconv[rows, lanes] = jnp.where(always, conv, hbuf[rows, lanes])
        v = aconv[rows, :]
        mu = jnp.mean(v, axis=-1, keepdims=True)
        cen = v - mu
        var = jnp.mean(cen * cen, axis=-1, keepdims=True)
        y = (cen * lax.rsqrt(var + EPS)) * lng_ref[...] + lnb_ref[...]
        mix[rows, 0:d_a] = (y * jax.nn.sigmoid(y)).astype(BF16)
        for c in range(n_groups):
            lanes = slice(c * V7X_LANES, (c + 1) * V7X_LANES)
            accb = _conv_short_rows(cbuf, cbw_ref, r0, lanes, lanes)
            mix[rows, d_a + c * V7X_LANES:d_a + (c + 1) * V7X_LANES] = (
                bgbuf[rows, lanes] * accb).astype(BF16)

    h1_ref[...] = x_ref[...] + jnp.dot(mix[...], wout_ref[...], preferred_element_type=F32)
    for rows in chunks:
        hn2_ref[rows, :] = _rmsnorm_rows(h1_ref[rows, :], g2_ref[...]).astype(BF16)


def _mix(x2, g1, w_in, caw, cab, lng, lnb, cbw, w_out, g2, seq, tm):
    t, d = x2.shape
    d_a = caw.shape[1]
    assert seq % tm == 0 and tm % ROW_CHUNK == 0 and w_in.shape[1] == 5 * d_a
    row = lambda i: (i, 0)
    wseg = lambda seg: pl.BlockSpec((d, d_a), lambda i, seg=seg: (0, seg),
                                    pipeline_mode=pl.Buffered(1))
    scratch = ([((tm, d), BF16), ((CONV_A_HALO + tm, d_a), F32), ((CONV_B_HALO + tm, d_a), F32)]
               + 4 * [((tm, d_a), F32)] + [((tm, d), BF16)])
    vmem = (2 * (2 * _nbytes((tm, d), F32) + _nbytes((tm, d), BF16))
            + _nbytes(w_in.shape, BF16) + _nbytes(w_out.shape, BF16)
            + sum(_nbytes(s, dt) for s, dt in scratch)
            + 4 * _nbytes((tm, d_a), F32) + (2 << 20))
    return pl.pallas_call(
        functools.partial(_mix_kernel, seq // tm),
        grid=(t // tm,),
        in_specs=[pl.BlockSpec((tm, d), row), _resident(g1.shape),
                  wseg(0), wseg(1), wseg(2), wseg(3), wseg(4),
                  _resident(caw.shape), _resident(cab.shape), _resident(lng.shape),
                  _resident(lnb.shape), _resident(cbw.shape), _resident(w_out.shape),
                  _resident(g2.shape)],
        out_specs=[pl.BlockSpec((tm, d), row), pl.BlockSpec((tm, d), row)],
        out_shape=[jax.ShapeDtypeStruct((t, d), F32), jax.ShapeDtypeStruct((t, d), BF16)],
        scratch_shapes=[pltpu.VMEM(s, dt) for s, dt in scratch],
        compiler_params=_params(("arbitrary",), vmem),
        name="mix",
    )(x2, g1, w_in, w_in, w_in, w_in, w_in, caw, cab, lng, lnb, cbw, w_out, g2)


def _ffn_up_kernel(n_m, tiles_per_seq, hn_ref, wg_ref, wu_ref, cg_ref, cu_ref, act_ref,
                   ubuf0, ubuf1, wg_bf, wu_bf):
    tm = hn_ref.shape[0]
    d, tf = wg_ref.shape
    halo = CONV_B_HALO
    s = pl.program_id(0)
    tile = jnp.minimum(s, pl.num_programs(0) - 2) % n_m
    first = (tile % tiles_per_seq) == 0
    always = s >= 0

    @pl.when(s == 0)
    def _():
        ubuf1[...] = jnp.zeros(ubuf1.shape, F32)

    @pl.when(tile == 0)
    def _():
        for k in range(d // WEIGHT_CAST_ROWS):
            rows = slice(k * WEIGHT_CAST_ROWS, (k + 1) * WEIGHT_CAST_ROWS)
            wg_bf[rows, :] = wg_ref[rows, :].astype(BF16)
            wu_bf[rows, :] = wu_ref[rows, :].astype(BF16)

    def step(cur, prev):
        cur[0:halo, :] = jnp.where(first, 0.0, prev[tm:tm + halo, :])
        cur[halo:, 0:tf] = jnp.dot(hn_ref[...], wg_bf[...], preferred_element_type=F32)
        cur[halo:, tf:] = jnp.dot(hn_ref[...], wu_bf[...], preferred_element_type=F32)
        for c in range(tf // V7X_LANES):
            lanes = slice(c * V7X_LANES, (c + 1) * V7X_LANES)
            ulanes = slice(tf + c * V7X_LANES, tf + (c + 1) * V7X_LANES)
            pace = (always, cur, lanes)
            for r in range(tm // ROW_CHUNK):
                r0 = r * ROW_CHUNK
                g = _conv_short_rows(prev, cg_ref, r0, lanes, lanes, pace)
                u = _conv_short_rows(prev, cu_ref, r0, ulanes, lanes)
                gated = jnp.where(always, (g * jax.nn.sigmoid(g)) * u,
                                  cur[halo + r0:halo + r0 + ROW_CHUNK, ulanes])
                act_ref[r0:r0 + ROW_CHUNK, lanes] = gated.astype(act_ref.dtype)

    @pl.when(s % 2 == 0)
    def _():
        step(ubuf0, ubuf1)

    @pl.when(s % 2 == 1)
    def _():
        step(ubuf1, ubuf0)


def _ffn_up(hn2, w_up, conv_w, seq, tm, tf):
    t, d = hn2.shape
    d_ff = w_up.shape[1] // 2
    n_f = d_ff // tf
    n_m = t // tm
    assert seq % tm == 0 and d_ff % tf == 0 and tm % ROW_CHUNK == 0
    n_steps = n_f * n_m
    cur = lambda s: jnp.minimum(s, n_steps - 1)
    lag = lambda s: jnp.maximum(s - 1, 0)
    ubuf = pltpu.VMEM((tm + CONV_B_HALO, 2 * tf), F32)
    w_bf = pltpu.VMEM((d, tf), BF16)
    assert w_up.dtype == F32 and d % WEIGHT_CAST_ROWS == 0
    vmem = (2 * (_nbytes((tm, d), BF16) + 2 * _nbytes((d, tf), F32) + _nbytes((tm, tf), BF16))
            + 2 * _nbytes((d, tf), BF16) + 2 * _nbytes((tm + CONV_B_HALO, 2 * tf), F32)
            + 2 * _nbytes((tm, tf), F32) + (2 << 20))
    return pl.pallas_call(
        functools.partial(_ffn_up_kernel, n_m, seq // tm),
        grid=(n_steps + 1,),
        in_specs=[pl.BlockSpec((tm, d), lambda s: (cur(s) % n_m, 0)),
                  pl.BlockSpec((d, tf), lambda s: (0, cur(s) // n_m)),
                  pl.BlockSpec((d, tf), lambda s: (0, n_f + cur(s) // n_m)),
                  pl.BlockSpec((conv_w.shape[0], tf), lambda s: (0, lag(s) // n_m)),
                  pl.BlockSpec((conv_w.shape[0], tf), lambda s: (0, n_f + lag(s) // n_m))],
        out_specs=pl.BlockSpec((tm, tf), lambda s: (lag(s) % n_m, lag(s) // n_m)),
        out_shape=jax.ShapeDtypeStruct((t, d_ff), BF16),
        scratch_shapes=[ubuf, ubuf, w_bf, w_bf],
        compiler_params=_params(("arbitrary",), vmem),
        name="ffn_up",
    )(hn2, w_up, w_up, conv_w, conv_w)


def _ffn_down_kernel(act_ref, w_ref, h1_ref, h2_ref):
    h2_ref[...] = h1_ref[...] + jnp.dot(act_ref[...], w_ref[...], preferred_element_type=F32)


def _ffn_down(act, w_down, h1, tm):
    t, d_ff = act.shape
    d = w_down.shape[1]
    row = lambda i: (i, 0)
    vmem = (2 * (_nbytes((tm, d_ff), BF16) + 2 * _nbytes((tm, d), F32))
            + _nbytes(w_down.shape, BF16) + _nbytes((tm, d), F32) + (2 << 20))
    return pl.pallas_call(
        _ffn_down_kernel,
        grid=(t // tm,),
        in_specs=[pl.BlockSpec((tm, d_ff), row), _resident(w_down.shape),
                  pl.BlockSpec((tm, d), row)],
        out_specs=pl.BlockSpec((tm, d), row),
        out_shape=jax.ShapeDtypeStruct((t, d), F32),
        compiler_params=_params(("arbitrary",), vmem),
        name="ffn_down",
    )(act, w_down, h1)


def _ple_kernel(h2_ref, p_ref, wgate_ref, bgate_ref, wproj_ref, gf_ref, out_ref):
    h2 = h2_ref[...]
    gate = jax.nn.sigmoid(
        jnp.dot(h2.astype(BF16), wgate_ref[...], preferred_element_type=F32) + bgate_ref[...])
    emb = jnp.dot(p_ref[...].astype(BF16), wproj_ref[...], preferred_element_type=F32)
    out_ref[...] = _rmsnorm_rows(h2 + emb * gate, gf_ref[...])


def _ple(h2, p2, w_gate, b_gate, w_proj, g_final, tm):
    t, d = h2.shape
    kp = p2.shape[1]
    row = lambda i: (i, 0)
    vmem = (2 * (2 * _nbytes((tm, d), F32) + _nbytes((tm, kp), F32))
            + _nbytes(w_gate.shape, BF16) + _nbytes(w_proj.shape, BF16)
            + 4 * _nbytes((tm, d), F32) + (2 << 20))
    return pl.pallas_call(
        _ple_kernel,
        grid=(t // tm,),
        in_specs=[pl.BlockSpec((tm, d), row), pl.BlockSpec((tm, kp), row),
                  _resident(w_gate.shape), _resident(b_gate.shape),
                  _resident(w_proj.shape), _resident(g_final.shape)],
        out_specs=pl.BlockSpec((tm, d), row),
        out_shape=jax.ShapeDtypeStruct((t, d), F32),
        compiler_params=_params(("arbitrary",), vmem),
        name="ple",
    )(h2, p2, w_gate, b_gate, w_proj, g_final)


def kernel(x, p, norm_mix_g, w_in, conv_a_w, conv_a_b, ln_a_g, ln_a_b, conv_b_w, w_out,
           norm_ffn_g, w_up, conv_ffn_w, w_down, w_ple_gate, b_ple_gate, w_ple_proj,
           norm_final_g):
    b, s, d = x.shape
    depth = w_in.shape[0]
    assert depth == 1, "the final rmsnorm is fused into the last layer's embedding-gate kernel"
    row = lambda v: v.reshape(1, -1)
    h = x.reshape(b * s, d)
    for i in range(depth):
        h1, hn2 = _mix(h, row(norm_mix_g[i]), w_in[i].astype(BF16), conv_a_w[i],
                       row(conv_a_b[i]), row(ln_a_g[i]), row(ln_a_b[i]), conv_b_w[i],
                       w_out[i].astype(BF16), row(norm_ffn_g[i]), seq=s, tm=256)
        act = _ffn_up(hn2, w_up[i], conv_ffn_w[i], seq=s, tm=1024, tf=512)
        h2 = _ffn_down(act, w_down[i].astype(BF16), h1, tm=512)
        h = _ple(h2, p[i].reshape(b * s, -1), w_ple_gate[i].astype(BF16), row(b_ple_gate[i]),
                 w_ple_proj[i].astype(BF16), row(norm_final_g), tm=512)
    return h.reshape(b, s, d)
```

```python
import functools

import jax
import jax.numpy as jnp
from jax import lax
from jax.experimental import pallas as pl
from jax.experimental.pallas import tpu as pltpu

EPS = 1e-6
F32 = jnp.float32
BF16 = jnp.bfloat16

V7X_VMEM_BYTES = 64 * 1024 * 1024
V7X_LANES = 128
F32_SUBLANES = 8

CONV_A_HALO = 32
CONV_B_HALO = F32_SUBLANES
ROW_CHUNK = 64
WEIGHT_CAST_ROWS = 256


def _params(sem, vmem_bytes):
    assert vmem_bytes <= V7X_VMEM_BYTES
    return pltpu.CompilerParams(dimension_semantics=sem, vmem_limit_bytes=int(vmem_bytes))


def _nbytes(shape, dtype):
    n = 1
    for s in shape:
        n *= s
    return n * jnp.dtype(dtype).itemsize


def _resident(shape):
    return pl.BlockSpec(shape, lambda *_: (0, 0), pipeline_mode=pl.Buffered(1))


def _rmsnorm_rows(x, g):
    ms = jnp.mean(x * x, axis=-1, keepdims=True)
    return (x * lax.rsqrt(ms + EPS)) * g


def _conv_a_rows(abuf, caw_ref, r0, lanes):
    ka = caw_ref.shape[0]
    base = CONV_A_HALO - (ka - 1)
    n = ROW_CHUNK + CONV_A_HALO
    v = abuf[r0:r0 + n, lanes]
    acc = None
    for r in range(F32_SUBLANES):
        vr = v if r == 0 else _rows_up(v, r)
        for q in range(CONV_A_HALO // F32_SUBLANES + 1):
            k = F32_SUBLANES * q + r - base
            if 0 <= k < ka:
                term = vr[F32_SUBLANES * q:F32_SUBLANES * q + ROW_CHUNK] * caw_ref[k:k + 1, lanes]
                acc = term if acc is None else acc + term
    return acc


def _conv_short_rows(ubuf, w_ref, r0, lanes, wlanes, pace=None):
    kf = w_ref.shape[0]
    halo = CONV_B_HALO
    assert kf - 1 <= halo
    n = ROW_CHUNK + halo
    v = ubuf[r0:r0 + n, lanes]
    if pace is not None:
        v = jnp.where(pace[0], v, pace[1][r0:r0 + n, pace[2]])
    acc = v[halo:] * w_ref[kf - 1:kf, wlanes]
    for k in range(kf - 1):
        back = kf - 1 - k
        acc = acc + _rows_down(v, back)[halo:] * w_ref[k:k + 1, wlanes]
    return acc


def _vregs(v):
    return [v[i:i + F32_SUBLANES] for i in range(0, v.shape[0], F32_SUBLANES)]


def _rows_down(v, k):
    b = _vregs(v)
    tail = lax.broadcasted_iota(jnp.int32, b[0].shape, 0) >= F32_SUBLANES - k
    return jnp.concatenate(
        [b[0]] + [pltpu.roll(jnp.where(tail, p, c), k, axis=0) for p, c in zip(b, b[1:])], axis=0)


def _rows_up(v, k):
    b = _vregs(v)
    head = lax.broadcasted_iota(jnp.int32, b[0].shape, 0) < k
    return jnp.concatenate(
        [pltpu.roll(jnp.where(head, nx, c), F32_SUBLANES - k, axis=0) for c, nx in zip(b, b[1:])]
        + [b[-1]], axis=0)


def _mix_kernel(tiles_per_seq, x_ref, g1_ref, wa_ref, wg_ref, wb_ref, wc_ref, wh_ref,
                caw_ref, cab_ref, lng_ref, lnb_ref, cbw_ref, wout_ref, g2_ref,
                h1_ref, hn2_ref, hn, abuf, cbuf, bgbuf, gbuf, hbuf, aconv, mix):
    tm = x_ref.shape[0]
    d_a = caw_ref.shape[1]
    n_groups = d_a // V7X_LANES
    first = (pl.program_id(0) % tiles_per_seq) == 0
    always = pl.program_id(0) >= 0
    chunks = [slice(r * ROW_CHUNK, (r + 1) * ROW_CHUNK) for r in range(tm // ROW_CHUNK)]

    for rows in chunks:
        hn[rows, :] = _rmsnorm_rows(x_ref[rows, :], g1_ref[...]).astype(BF16)

    abuf[0:CONV_A_HALO, :] = jnp.where(first, 0.0, abuf[tm:tm + CONV_A_HALO, :])
    cbuf[0:CONV_B_HALO, :] = jnp.where(first, 0.0, cbuf[tm:tm + CONV_B_HALO, :])

    dot = lambda w_ref: jnp.dot(hn[...], w_ref[...], preferred_element_type=F32)
    abuf[CONV_A_HALO:, :] = dot(wa_ref)
    gbuf[...] = dot(wg_ref)
    for rows in chunks:
        arows = slice(CONV_A_HALO + rows.start, CONV_A_HALO + rows.stop)
        abuf[arows, :] = abuf[arows, :] * jax.nn.sigmoid(gbuf[rows, :])
    bgbuf[...] = dot(wb_ref)
    cbuf[CONV_B_HALO:, :] = dot(wc_ref)
    hbuf[...] = dot(wh_ref)
    for rows in chunks:
        crows = slice(CONV_B_HALO + rows.start, CONV_B_HALO + rows.stop)
        cbuf[crows, :] = cbuf[crows, :] * hbuf[rows, :]

    for rows in chunks:
        for c in range(n_groups):
            lanes = slice(c * V7X_LANES, (c + 1) * V7X_LANES)
            accb = _conv_short_rows(cbuf, cbw_ref, rows.start, lanes, lanes)
            mix[rows, d_a + c * V7X_LANES:d_a + (c + 1) * V7X_LANES] = (
                bgbuf[rows, lanes] * accb).astype(BF16)

    h1_ref[...] = x_ref[...] + jnp.dot(mix[:, d_a:], wout_ref[d_a:, :],
                                       preferred_element_type=F32)

    for k, rows in enumerate(chunks):
        r0 = rows.start
        hold = hbuf if 2 * k < len(chunks) else h1_ref
        for c in range(n_groups):
            lanes = slice(c * V7X_LANES, (c + 1) * V7X_LANES)
            conv = _conv_a_rows(abuf, caw_ref, r0, lanes) + cab_ref[:, lanes]
            aconv[rows, lanes] = jnp.where(always, conv, hold[rows, lanes])
        v = aconv[rows, :]
        mu = jnp.mean(v, axis=-1, keepdims=True)
        cen = v - mu
        var = jnp.mean(cen * cen, axis=-1, keepdims=True)
        y = (cen * lax.rsqrt(var + EPS)) * lng_ref[...] + lnb_ref[...]
        mix[rows, 0:d_a] = (y * jax.nn.sigmoid(y)).astype(BF16)

    h1_ref[...] += jnp.dot(mix[:, 0:d_a], wout_ref[0:d_a, :], preferred_element_type=F32)
    for rows in chunks:
        hn2_ref[rows, :] = _rmsnorm_rows(h1_ref[rows, :], g2_ref[...]).astype(BF16)


def _mix(x2, g1, w_in, caw, cab, lng, lnb, cbw, w_out, g2, seq, tm):
    t, d = x2.shape
    d_a = caw.shape[1]
    assert seq % tm == 0 and tm % ROW_CHUNK == 0 and w_in.shape[1] == 5 * d_a
    row = lambda i: (i, 0)
    wseg = lambda seg: pl.BlockSpec((d, d_a), lambda i, seg=seg: (0, seg),
                                    pipeline_mode=pl.Buffered(1))
    scratch = ([((tm, d), BF16), ((CONV_A_HALO + tm, d_a), F32), ((CONV_B_HALO + tm, d_a), F32)]
               + 4 * [((tm, d_a), F32)] + [((tm, d), BF16)])
    vmem = (2 * (2 * _nbytes((tm, d), F32) + _nbytes((tm, d), BF16))
            + _nbytes(w_in.shape, BF16) + _nbytes(w_out.shape, BF16)
            + sum(_nbytes(s, dt) for s, dt in scratch)
            + 4 * _nbytes((tm, d_a), F32) + (2 << 20))
    return pl.pallas_call(
        functools.partial(_mix_kernel, seq // tm),
        grid=(t // tm,),
        in_specs=[pl.BlockSpec((tm, d), row), _resident(g1.shape),
                  wseg(0), wseg(1), wseg(2), wseg(3), wseg(4),
                  _resident(caw.shape), _resident(cab.shape), _resident(lng.shape),
                  _resident(lnb.shape), _resident(cbw.shape), _resident(w_out.shape),
                  _resident(g2.shape)],
        out_specs=[pl.BlockSpec((tm, d), row), pl.BlockSpec((tm, d), row)],
        out_shape=[jax.ShapeDtypeStruct((t, d), F32), jax.ShapeDtypeStruct((t, d), BF16)],
        scratch_shapes=[pltpu.VMEM(s, dt) for s, dt in scratch],
        compiler_params=_params(("arbitrary",), vmem),
        name="mix",
    )(x2, g1, w_in, w_in, w_in, w_in, w_in, caw, cab, lng, lnb, cbw, w_out, g2)


def _ffn_up_kernel(n_m, tiles_per_seq, hn_ref, wg_ref, wu_ref, cg_ref, cu_ref, act_ref,
                   ubuf0, ubuf1, wg_bf, wu_bf):
    tm = hn_ref.shape[0]
    d, tf = wg_ref.shape
    halo = CONV_B_HALO
    s = pl.program_id(0)
    tile = jnp.minimum(s, pl.num_programs(0) - 2) % n_m
    first = (tile % tiles_per_seq) == 0
    always = s >= 0

    @pl.when(s == 0)
    def _():
        ubuf1[...] = jnp.zeros(ubuf1.shape, F32)

    @pl.when(tile == 0)
    def _():
        for k in range(d // WEIGHT_CAST_ROWS):
            rows = slice(k * WEIGHT_CAST_ROWS, (k + 1) * WEIGHT_CAST_ROWS)
            wg_bf[rows, :] = wg_ref[rows, :].astype(BF16)
            wu_bf[rows, :] = wu_ref[rows, :].astype(BF16)

    def step(cur, prev):
        cur[0:halo, :] = jnp.where(first, 0.0, prev[tm:tm + halo, :])
        cur[halo:, 0:tf] = jnp.dot(hn_ref[...], wg_bf[...], preferred_element_type=F32)
        cur[halo:, tf:] = jnp.dot(hn_ref[...], wu_bf[...], preferred_element_type=F32)
        for c in range(tf // V7X_LANES):
            lanes = slice(c * V7X_LANES, (c + 1) * V7X_LANES)
            ulanes = slice(tf + c * V7X_LANES, tf + (c + 1) * V7X_LANES)
            pace = (always, cur, lanes)
            for r in range(tm // ROW_CHUNK):
                r0 = r * ROW_CHUNK
                g = _conv_short_rows(prev, cg_ref, r0, lanes, lanes, pace)
                u = _conv_short_rows(prev, cu_ref, r0, ulanes, lanes)
                gated = jnp.where(always, (g * jax.nn.sigmoid(g)) * u,
                                  cur[halo + r0:halo + r0 + ROW_CHUNK, ulanes])
                act_ref[r0:r0 + ROW_CHUNK, lanes] = gated.astype(act_ref.dtype)

    @pl.when(s % 2 == 0)
    def _():
        step(ubuf0, ubuf1)

    @pl.when(s % 2 == 1)
    def _():
        step(ubuf1, ubuf0)


def _ffn_up(hn2, w_up, conv_w, seq, tm, tf):
    t, d = hn2.shape
    d_ff = w_up.shape[1] // 2
    n_f = d_ff // tf
    n_m = t // tm
    assert seq % tm == 0 and d_ff % tf == 0 and tm % ROW_CHUNK == 0
    n_steps = n_f * n_m
    cur = lambda s: jnp.minimum(s, n_steps - 1)
    lag = lambda s: jnp.maximum(s - 1, 0)
    ubuf = pltpu.VMEM((tm + CONV_B_HALO, 2 * tf), F32)
    w_bf = pltpu.VMEM((d, tf), BF16)
    assert w_up.dtype == F32 and d % WEIGHT_CAST_ROWS == 0
    vmem = (2 * (_nbytes((tm, d), BF16) + 2 * _nbytes((d, tf), F32) + _nbytes((tm, tf), BF16))
            + 2 * _nbytes((d, tf), BF16) + 2 * _nbytes((tm + CONV_B_HALO, 2 * tf), F32)
            + 2 * _nbytes((tm, tf), F32) + (2 << 20))
    return pl.pallas_call(
        functools.partial(_ffn_up_kernel, n_m, seq // tm),
        grid=(n_steps + 1,),
        in_specs=[pl.BlockSpec((tm, d), lambda s: (cur(s) % n_m, 0)),
                  pl.BlockSpec((d, tf), lambda s: (0, cur(s) // n_m)),
                  pl.BlockSpec((d, tf), lambda s: (0, n_f + cur(s) // n_m)),
                  pl.BlockSpec((conv_w.shape[0], tf), lambda s: (0, lag(s) // n_m)),
                  pl.BlockSpec((conv_w.shape[0], tf), lambda s: (0, n_f + lag(s) // n_m))],
        out_specs=pl.BlockSpec((tm, tf), lambda s: (lag(s) % n_m, lag(s) // n_m)),
        out_shape=jax.ShapeDtypeStruct((t, d_ff), BF16),
        scratch_shapes=[ubuf, ubuf, w_bf, w_bf],
        compiler_params=_params(("arbitrary",), vmem),
        name="ffn_up",
    )(hn2, w_up, w_up, conv_w, conv_w)


def _ffn_down_kernel(act_ref, w_ref, h1_ref, h2_ref):
    h2_ref[...] = h1_ref[...] + jnp.dot(act_ref[...], w_ref[...], preferred_element_type=F32)


def _ffn_down(act, w_down, h1, tm):
    t, d_ff = act.shape
    d = w_down.shape[1]
    row = lambda i: (i, 0)
    vmem = (2 * (_nbytes((tm, d_ff), BF16) + 2 * _nbytes((tm, d), F32))
            + _nbytes(w_down.shape, BF16) + _nbytes((tm, d), F32) + (2 << 20))
    return pl.pallas_call(
        _ffn_down_kernel,
        grid=(t // tm,),
        in_specs=[pl.BlockSpec((tm, d_ff), row), _resident(w_down.shape),
                  pl.BlockSpec((tm, d), row)],
        out_specs=pl.BlockSpec((tm, d), row),
        out_shape=jax.ShapeDtypeStruct((t, d), F32),
        compiler_params=_params(("arbitrary",), vmem),
        name="ffn_down",
    )(act, w_down, h1)


def _ple_kernel(h2_ref, p_ref, wgate_ref, bgate_ref, wproj_ref, gf_ref, out_ref):
    h2 = h2_ref[...]
    gate = jax.nn.sigmoid(
        jnp.dot(h2.astype(BF16), wgate_ref[...], preferred_element_type=F32) + bgate_ref[...])
    emb = jnp.dot(p_ref[...].astype(BF16), wproj_ref[...], preferred_element_type=F32)
    out_ref[...] = _rmsnorm_rows(h2 + emb * gate, gf_ref[...])


def _ple(h2, p2, w_gate, b_gate, w_proj, g_final, tm):
    t, d = h2.shape
    kp = p2.shape[1]
    row = lambda i: (i, 0)
    vmem = (2 * (2 * _nbytes((tm, d), F32) + _nbytes((tm, kp), F32))
            + _nbytes(w_gate.shape, BF16) + _nbytes(w_proj.shape, BF16)
            + 4 * _nbytes((tm, d), F32) + (2 << 20))
    return pl.pallas_call(
        _ple_kernel,
        grid=(t // tm,),
        in_specs=[pl.BlockSpec((tm, d), row), pl.BlockSpec((tm, kp), row),
                  _resident(w_gate.shape), _resident(b_gate.shape),
                  _resident(w_proj.shape), _resident(g_final.shape)],
        out_specs=pl.BlockSpec((tm, d), row),
        out_shape=jax.ShapeDtypeStruct((t, d), F32),
        compiler_params=_params(("arbitrary",), vmem),
        name="ple",
    )(h2, p2, w_gate, b_gate, w_proj, g_final)


def kernel(x, p, norm_mix_g, w_in, conv_a_w, conv_a_b, ln_a_g, ln_a_b, conv_b_w, w_out,
           norm_ffn_g, w_up, conv_ffn_w, w_down, w_ple_gate, b_ple_gate, w_ple_proj,
           norm_final_g):
    b, s, d = x.shape
    depth = w_in.shape[0]
    assert depth == 1, "the final rmsnorm is fused into the last layer's embedding-gate kernel"
    row = lambda v: v.reshape(1, -1)
    h = x.reshape(b * s, d)
    for i in range(depth):
        h1, hn2 = _mix(h, row(norm_mix_g[i]), w_in[i].astype(BF16), conv_a_w[i],
                       row(conv_a_b[i]), row(ln_a_g[i]), row(ln_a_b[i]), conv_b_w[i],
                       w_out[i].astype(BF16), row(norm_ffn_g[i]), seq=s, tm=256)
        act = _ffn_up(hn2, w_up[i], conv_ffn_w[i], seq=s, tm=1024, tf=512)
        h2 = _ffn_down(act, w_down[i].astype(BF16), h1, tm=512)
        h = _ple(h2, p[i].reshape(b * s, -1), w_ple_gate[i].astype(BF16), row(b_ple_gate[i]),
                 w_ple_proj[i].astype(BF16), row(norm_final_g), tm=512)
    return h.reshape(b, s, d)
```
